```python
import jax
import jax.numpy as jnp
from jax import lax
import numpy as np

D_MODEL = 1024
BATCH = 4
SEQ = 4096
DEPTH = 2
DEC_BATCH = 128
DEC_SEQ = 8
PAST_LEN = 2048
PAGE_SIZE = 128

N_EVEN_LAYERS = (DEPTH + 1) // 2
N_ODD_LAYERS = DEPTH // 2
PLE_DIM = 256
HEAD_DIM = 64
RWKV_HEADS = 8
RWKV_WIDTH = RWKV_HEADS * HEAD_DIM
W_LORA = 64
A_LORA = 64
G_LORA = 128
RWKV_PROJ = 3 * RWKV_WIDTH + W_LORA + A_LORA + G_LORA
RWKV_LN_EPS = 64e-5
MOBA_HEADS = 8
MOBA_WIDTH = MOBA_HEADS * HEAD_DIM
MOBA_BLOCK = 256
MOBA_TOPK = 3
MOBA_QCHUNK = 128
A_IN = RWKV_PROJ + 3 * MOBA_WIDTH
A_OUT = RWKV_WIDTH + MOBA_WIDTH
RET_HEADS = 4
RET_DK = 256
RET_DV = 512
RET_QK = RET_HEADS * RET_DK
RET_V = RET_HEADS * RET_DV
C_IN = 2 * RET_QK + 2 * RET_V
RET_CHUNK = 128
ROPE_BASE = 10000.0
D_FF = ((-(-8 * D_MODEL // 3) + 255) // 256) * 256
NORM_EPS = 1e-6
GN_EPS = 1e-5
F32 = jnp.float32

kernel_name = 'rwkv7_moba_retnet_hybrid_step'


def rms_norm(x, w):
    x32 = x.astype(F32)
    y = x32 * lax.rsqrt(jnp.mean(x32 * x32, axis=-1, keepdims=True) + NORM_EPS)
    return (y * w.astype(F32)).astype(x.dtype)


def head_layer_norm(x, eps):
    x32 = x.astype(F32)
    mu = jnp.mean(x32, axis=-1, keepdims=True)
    var = jnp.mean(jnp.square(x32 - mu), axis=-1, keepdims=True)
    return (x32 - mu) * lax.rsqrt(var + eps)


def swiglu(x, w_gate, w_up, w_down):
    return (jax.nn.silu(x @ w_gate) * (x @ w_up)) @ w_down


def rotary(x, pos):
    half = x.shape[-1] // 2
    inv = ROPE_BASE ** (-jnp.arange(half, dtype=F32) / half)
    ang = pos[:, None] * inv[None, :]
    cos = jnp.cos(ang)[None, :, None, :]
    sin = jnp.sin(ang)[None, :, None, :]
    x1, x2 = x[..., :half], x[..., half:]
    return jnp.concatenate([x1 * cos - x2 * sin, x1 * sin + x2 * cos], axis=-1)


def _rwkv7_step(S, inp):
    r, w, k, v, a, b = inp
    sa = jnp.einsum('bhvk,bhk->bhv', S, a)
    S = S * w[:, :, None, :] + sa[..., None] * b[:, :, None, :] + v[..., None] * k[:, :, None, :]
    return S, jnp.einsum('bhvk,bhk->bhv', S, r)


def rwkv7_time_mix(zr, shift0, S0, mu, w0, w2, a0, a2, g2, k_k, k_a, r_k, ln_w, ln_b):
    B, T, _ = zr.shape
    prev = jnp.concatenate([shift0[:, None, :].astype(zr.dtype), zr[:, :-1]], axis=1)
    zm = zr + mu * (prev - zr)
    rw = RWKV_WIDTH
    r, k, v, wd, ad, gd = jnp.split(zm, [rw, 2 * rw, 3 * rw, 3 * rw + W_LORA, 3 * rw + W_LORA + A_LORA], axis=-1)
    logw = -jax.nn.softplus(-(w0 + jnp.tanh(wd) @ w2)) - 0.5
    decay = jnp.exp(-jnp.exp(logw.astype(F32)))
    a = jax.nn.sigmoid(a0 + ad @ a2)
    g = jax.nn.sigmoid(gd) @ g2
    heads = lambda t: t.astype(F32).reshape(B, T, RWKV_HEADS, HEAD_DIM)
    kk = heads(k * k_k)
    kk = kk / jnp.maximum(jnp.linalg.norm(kk, axis=-1, keepdims=True), 1e-12)
    k = k * (1.0 + (a - 1.0) * k_a)
    rh, kh, vh, wh, ah = heads(r), heads(k), heads(v), heads(decay), heads(a)
    tm = lambda t: jnp.moveaxis(t, 1, 0)
    S_T, ys = lax.scan(_rwkv7_step, S0.astype(F32), (tm(rh), tm(wh), tm(kh), tm(vh), tm(-kk), tm(kk * ah)))
    y = jnp.moveaxis(ys, 0, 1)
    y = head_layer_norm(y, RWKV_LN_EPS).reshape(B, T, rw) * ln_w + ln_b
    bonus = jnp.sum(rh * kh * r_k, axis=-1, keepdims=True) * vh
    out = (y + bonus.reshape(B, T, rw)) * g
    return out, S_T, zr[:, -1]


def moba_attention(q, k_all, v_all, pos0):
    B, Tq, H, Dh = q.shape
    Tk = k_all.shape[1]
    nb = -(-Tk // MOBA_BLOCK)
    pad = nb * MOBA_BLOCK - Tk
    to_blocks = lambda t: jnp.pad(t, ((0, 0), (0, pad), (0, 0), (0, 0))).reshape(
        B, nb, MOBA_BLOCK, H, Dh).transpose(0, 3, 1, 2, 4)
    kb, vb = to_blocks(k_all), to_blocks(v_all)
    k_mean = jnp.mean(kb.astype(F32), axis=3)
    n_sel = min(MOBA_TOPK, nb)
    qc = MOBA_QCHUNK if Tq % MOBA_QCHUNK == 0 else Tq
    n_chunks = Tq // qc
    q_pos = pos0 + jnp.arange(Tq, dtype=jnp.int32)
    q_chunks = q.transpose(0, 2, 1, 3).reshape(B, H, n_chunks, qc, Dh).transpose(2, 0, 1, 3, 4)
    gather = jax.vmap(jax.vmap(lambda blocks, idx: blocks[idx]))
    scale = HEAD_DIM ** -0.5

    def attend_chunk(args):
        qb, pb = args
        blk = pb // MOBA_BLOCK
        gate = jnp.einsum('bhqd,bhnd->bhqn', qb.astype(F32), k_mean)
        is_past = jnp.arange(nb)[None, :] < blk[:, None]
        gate = jnp.where(is_past, gate, -jnp.inf)
        _, sel = lax.top_k(gate, n_sel)
        sel_ok = sel < blk[:, None]
        scores = []
        for s in range(n_sel):
            sc = jnp.einsum('bhqd,bhqjd->bhqj', qb, gather(kb, sel[..., s])).astype(F32) * scale
            scores.append(jnp.where(sel_ok[..., s, None], sc, -jnp.inf))
        key_pos = blk[:, None] * MOBA_BLOCK + jnp.arange(MOBA_BLOCK)[None, :]
        sc = jnp.einsum('bhqd,bhqjd->bhqj', qb, kb[:, :, blk]).astype(F32) * scale
        scores.append(jnp.where(key_pos <= pb[:, None], sc, -jnp.inf))
        probs = jax.nn.softmax(jnp.concatenate(scores, axis=-1), axis=-1)
        probs = probs.reshape(B, H, qc, n_sel + 1, MOBA_BLOCK)
        out = jnp.einsum('bhqj,bhqjd->bhqd', probs[..., n_sel, :], vb[:, :, blk])
        for s in range(n_sel):
            out = out + jnp.einsum('bhqj,bhqjd->bhqd', probs[..., s, :], gather(vb, sel[..., s]))
        return out

    out = lax.map(attend_chunk, (q_chunks, q_pos.reshape(n_chunks, qc)))
    return out.transpose(1, 0, 3, 2, 4).reshape(B, Tq, H * Dh)


def even_layer_mixer(xn, k_past, v_past, S0, shift0, w_in, mu, w0, w2, a0, a2, g2,
                     k_k, k_a, r_k, ln_w, ln_b, w_out):
    B, T, _ = xn.shape
    z = xn @ w_in
    zr, zq, zk, zv = jnp.split(z, [RWKV_PROJ, RWKV_PROJ + MOBA_WIDTH, RWKV_PROJ + 2 * MOBA_WIDTH], axis=-1)
    rwkv_out, S_T, shift_T = rwkv7_time_mix(zr, shift0, S0, mu, w0, w2, a0, a2, g2,
                                            k_k, k_a, r_k, ln_w, ln_b)
    heads = lambda t: t.reshape(B, T, MOBA_HEADS, HEAD_DIM)
    k_new, v_new = heads(zk), heads(zv)
    k_all = jnp.concatenate([k_past.astype(k_new.dtype), k_new], axis=1)
    v_all = jnp.concatenate([v_past.astype(v_new.dtype), v_new], axis=1)
    moba_out = moba_attention(heads(zq), k_all, v_all, k_past.shape[1])
    out = jnp.concatenate([rwkv_out, moba_out], axis=-1) @ w_out
    return out, k_new, v_new, S_T, shift_T


def chunk_retention(q, k, v, S0):
    B, T, H, _ = q.shape
    Dv = v.shape[-1]
    C = RET_CHUNK if T % RET_CHUNK == 0 else T
    n = T // C
    log_g = jnp.log1p(-jnp.exp2(-5.0 - jnp.arange(H, dtype=F32)))
    i = jnp.arange(C, dtype=F32)
    diff = i[:, None] - i[None, :]
    dmask = jnp.where(diff[None] >= 0, jnp.exp(jnp.maximum(diff, 0.0)[None] * log_g[:, None, None]), 0.0)
    cross = jnp.exp((i + 1.0)[:, None] * log_g[None, :])[None, :, :, None]
    kdec = jnp.exp((C - 1.0 - i)[:, None] * log_g[None, :])[None, :, :, None]
    sdec = jnp.exp(C * log_g)[None, :, None, None]
    to_chunks = lambda t: t.astype(F32).reshape(B, n, C, H, t.shape[-1]).swapaxes(0, 1)

    def step(S, inp):
        qc, kc, vc = inp
        att = jnp.einsum('bihd,bjhd->bhij', qc, kc) * dmask
        y = jnp.einsum('bhij,bjhe->bihe', att, vc) + jnp.einsum('bihd,bhde->bihe', qc, S) * cross
        S = S * sdec + jnp.einsum('bjhd,bjhe->bhde', kc * kdec, vc)
        return S, y

    S_T, ys = lax.scan(step, S0.astype(F32), (to_chunks(q), to_chunks(k), to_chunks(v)))
    return ys.swapaxes(0, 1).reshape(B, T, H, Dv), S_T


def odd_layer_mixer(xn, S0, pos0, w_in, gn_w, w_out):
    B, T, _ = xn.shape
    z = xn @ w_in
    q, k, v, g = jnp.split(z, [RET_QK, 2 * RET_QK, 2 * RET_QK + RET_V], axis=-1)
    pos = (pos0 + jnp.arange(T)).astype(F32)
    q = rotary(q.reshape(B, T, RET_HEADS, RET_DK), pos)
    k = rotary(k.reshape(B, T, RET_HEADS, RET_DK), pos) * (RET_DK ** -0.5)
    y, S_T = chunk_retention(q, k, v.reshape(B, T, RET_HEADS, RET_DV), S0)
    y = head_layer_norm(y, GN_EPS).reshape(B, T, RET_V) * gn_w
    return (jax.nn.silu(g) * y) @ w_out, S_T


def run_trunk(x, p, pos0, past_kv, rwkv_S0, rwkv_shift0, ret_S0, W):
    h = x
    k_rows, v_rows, rwkv_S, rwkv_shift, ret_S = [], [], [], [], []
    for i in range(DEPTH):
        li = i // 2
        xn = rms_norm(h, W['norm_mix'][i])
        if i % 2 == 0:
            k_past, v_past = past_kv(li)
            mix, k_new, v_new, S_T, sh_T = even_layer_mixer(
                xn, k_past, v_past, rwkv_S0[li], rwkv_shift0[li], W['a_w_in'][li], W['rwkv_mu'][li],
                W['rwkv_w0'][li], W['rwkv_w2'][li], W['rwkv_a0'][li], W['rwkv_a2'][li], W['rwkv_g2'][li],
                W['rwkv_k_k'][li], W['rwkv_k_a'][li], W['rwkv_r_k'][li], W['rwkv_ln_w'][li],
                W['rwkv_ln_b'][li], W['a_w_out'][li])
            k_rows.append(k_new)
            v_rows.append(v_new)
            rwkv_S.append(S_T)
            rwkv_shift.append(sh_T)
        else:
            mix, S_T = odd_layer_mixer(xn, ret_S0[li], pos0, W['c_w_in'][li], W['ret_gn_w'][li], W['c_w_out'][li])
            ret_S.append(S_T)
        h = h + mix
        h = h + swiglu(rms_norm(h, W['norm_ffn'][i]), W['ffn_w_gate'][i], W['ffn_w_up'][i], W['ffn_w_down'][i])
        gate = jax.nn.sigmoid(rms_norm(h, W['ple_norm'][i]) @ W['ple_gate'][i])
        h = h + (p[i] @ W['ple_proj'][i]) * gate
    y = rms_norm(h, W['final_norm'])
    return y, jnp.stack(k_rows), jnp.stack(v_rows), jnp.stack(rwkv_S), jnp.stack(rwkv_shift), jnp.stack(ret_S)


def setup_inputs(seed: int = 0) -> dict:
    key = jax.random.key(seed)
    ks = iter(jax.random.split(key, 64))

    def nrm(shape, scale):
        return jax.random.normal(next(ks), shape, F32) * scale

    def gain(shape, s=0.02):
        return 1.0 + nrm(shape, s)

    NE, NO = N_EVEN_LAYERS, N_ODD_LAYERS
    n_pages = PAST_LEN // PAGE_SIZE
    n_pool = (5 * DEC_BATCH * n_pages + 3) // 4
    page_table = jax.random.permutation(next(ks), n_pool)[: DEC_BATCH * n_pages].reshape(
        DEC_BATCH, n_pages).astype(jnp.int32)
    return {
        'x_prompt': nrm((BATCH, SEQ, D_MODEL), 1.0),
        'x_sample': nrm((DEC_BATCH, DEC_SEQ, D_MODEL), 1.0),
        'cache_moba_k': nrm((NE, n_pool, PAGE_SIZE, MOBA_HEADS, HEAD_DIM), 1.0),
        'cache_moba_v': nrm((NE, n_pool, PAGE_SIZE, MOBA_HEADS, HEAD_DIM), 1.0),
        'state_rwkv': nrm((NE, DEC_BATCH, RWKV_HEADS, HEAD_DIM, HEAD_DIM), 0.5),
        'state_rwkv_shift': nrm((NE, DEC_BATCH, RWKV_PROJ), 1.0),
        'state_ret': nrm((NO, DEC_BATCH, RET_HEADS, RET_DK, RET_DV), 0.1),
        'page_table': page_table,
        'p_prompt': nrm((DEPTH, BATCH, SEQ, PLE_DIM), 1.0),
        'p_sample': nrm((DEPTH, DEC_BATCH, DEC_SEQ, PLE_DIM), 1.0),
        'norm_mix': gain((DEPTH, D_MODEL)),
        'norm_ffn': gain((DEPTH, D_MODEL)),
        'ffn_w_gate': nrm((DEPTH, D_MODEL, D_FF), D_MODEL ** -0.5),
        'ffn_w_up': nrm((DEPTH, D_MODEL, D_FF), D_MODEL ** -0.5),
        'ffn_w_down': nrm((DEPTH, D_FF, D_MODEL), D_FF ** -0.5),
        'ple_norm': gain((DEPTH, D_MODEL)),
        'ple_gate': nrm((DEPTH, D_MODEL, D_MODEL), D_MODEL ** -0.5),
        'ple_proj': nrm((DEPTH, PLE_DIM, D_MODEL), PLE_DIM ** -0.5),
        'a_w_in': nrm((NE, D_MODEL, A_IN), D_MODEL ** -0.5),
        'rwkv_mu': jax.random.uniform(next(ks), (NE, RWKV_PROJ), F32),
        'rwkv_w0': jax.random.uniform(next(ks), (NE, RWKV_WIDTH), F32, -6.0, 1.0),
        'rwkv_w2': nrm((NE, W_LORA, RWKV_WIDTH), 0.1),
        'rwkv_a0': nrm((NE, RWKV_WIDTH), 0.1),
        'rwkv_a2': nrm((NE, A_LORA, RWKV_WIDTH), 0.1),
        'rwkv_g2': nrm((NE, G_LORA, RWKV_WIDTH), G_LORA ** -0.5),
        'rwkv_k_k': gain((NE, RWKV_WIDTH), 0.1),
        'rwkv_k_a': gain((NE, RWKV_WIDTH), 0.1),
        'rwkv_r_k': nrm((NE, RWKV_HEADS, HEAD_DIM), 0.1),
        'rwkv_ln_w': gain((NE, RWKV_WIDTH)),
        'rwkv_ln_b': nrm((NE, RWKV_WIDTH), 0.02),
        'a_w_out': nrm((NE, A_OUT, D_MODEL), A_OUT ** -0.5),
        'c_w_in': nrm((NO, D_MODEL, C_IN), D_MODEL ** -0.5),
        'ret_gn_w': gain((NO, RET_V)),
        'c_w_out': nrm((NO, RET_V, D_MODEL), RET_V ** -0.5),
        'final_norm': gain((D_MODEL,)),
    }


def reference(x_prompt, x_sample, cache_moba_k, cache_moba_v, state_rwkv, state_rwkv_shift, state_ret,
              page_table, p_prompt, p_sample, norm_mix, norm_ffn, ffn_w_gate, ffn_w_up, ffn_w_down,
              ple_norm, ple_gate, ple_proj, a_w_in, rwkv_mu, rwkv_w0, rwkv_w2, rwkv_a0, rwkv_a2, rwkv_g2,
              rwkv_k_k, rwkv_k_a, rwkv_r_k, rwkv_ln_w, rwkv_ln_b, a_w_out, c_w_in, ret_gn_w, c_w_out,
              final_norm):
    W = dict(norm_mix=norm_mix, norm_ffn=norm_ffn, ffn_w_gate=ffn_w_gate, ffn_w_up=ffn_w_up,
             ffn_w_down=ffn_w_down, ple_norm=ple_norm, ple_gate=ple_gate, ple_proj=ple_proj,
             a_w_in=a_w_in, rwkv_mu=rwkv_mu, rwkv_w0=rwkv_w0, rwkv_w2=rwkv_w2, rwkv_a0=rwkv_a0,
             rwkv_a2=rwkv_a2, rwkv_g2=rwkv_g2, rwkv_k_k=rwkv_k_k, rwkv_k_a=rwkv_k_a, rwkv_r_k=rwkv_r_k,
             rwkv_ln_w=rwkv_ln_w, rwkv_ln_b=rwkv_ln_b, a_w_out=a_w_out, c_w_in=c_w_in,
             ret_gn_w=ret_gn_w, c_w_out=c_w_out, final_norm=final_norm)

    B = x_prompt.shape[0]
    empty = jnp.zeros((B, 0, MOBA_HEADS, HEAD_DIM), x_prompt.dtype)
    y_prompt, k_rows_prompt, v_rows_prompt, rwkv_state_prompt, rwkv_shift_prompt, ret_state_prompt = run_trunk(
        x_prompt, p_prompt, 0, lambda li: (empty, empty),
        jnp.zeros((N_EVEN_LAYERS, B, RWKV_HEADS, HEAD_DIM, HEAD_DIM), F32),
        jnp.zeros((N_EVEN_LAYERS, B, RWKV_PROJ), x_prompt.dtype),
        jnp.zeros((N_ODD_LAYERS, B, RET_HEADS, RET_DK, RET_DV), F32), W)

    Bd = x_sample.shape[0]
    n_rows = page_table.shape[1] * cache_moba_k.shape[2]

    def sample_past(li):
        gather = lambda pool: pool[li][page_table].reshape(Bd, n_rows, MOBA_HEADS, HEAD_DIM)
        return gather(cache_moba_k), gather(cache_moba_v)

    y_sample, k_rows_sample, v_rows_sample, rwkv_state_sample, rwkv_shift_sample, ret_state_sample = run_trunk(
        x_sample, p_sample, n_rows, sample_past, state_rwkv, state_rwkv_shift, state_ret, W)

    return (y_prompt, y_sample, k_rows_prompt, v_rows_prompt, k_rows_sample, v_rows_sample,
            rwkv_state_prompt, rwkv_state_sample, rwkv_shift_prompt, rwkv_shift_sample,
            ret_state_prompt, ret_state_sample)
```

```python
import functools
import math

import numpy as np
import jax
import jax.numpy as jnp
from jax import lax
from jax.experimental import pallas as pl
from jax.experimental.pallas import tpu as pltpu

F32 = jnp.float32
BF16 = jnp.bfloat16
HI = lax.Precision.HIGHEST

D_MODEL = 1024
DEPTH = 2
PLE_DIM = 256
HEAD_DIM = 64
RWKV_HEADS = 8
RWKV_WIDTH = RWKV_HEADS * HEAD_DIM
W_LORA = 64
A_LORA = 64
G_LORA = 128
RWKV_PROJ = 3 * RWKV_WIDTH + W_LORA + A_LORA + G_LORA
RWKV_LN_EPS = 64e-5
RWKV_CHUNK = 64
MOBA_HEADS = 8
MOBA_WIDTH = MOBA_HEADS * HEAD_DIM
MOBA_BLOCK = 256
MOBA_TOPK = 3
MOBA_QCHUNK = 128
A_IN = RWKV_PROJ + 3 * MOBA_WIDTH
RET_HEADS = 4
RET_DK = 256
RET_DV = 512
RET_QK = RET_HEADS * RET_DK
RET_V = RET_HEADS * RET_DV
C_IN = 2 * RET_QK + 2 * RET_V
RET_CHUNK = 128
ROPE_BASE = 10000.0
D_FF = 2816
NORM_EPS = 1e-6
GN_EPS = 1e-5

LANE = 128
VMEM_LIMIT = 56 * 1024 * 1024

NT = (((1,), (1,)), ((), ()))
TN = (((0,), (0,)), ((), ()))


def _params(*sem):
    return pltpu.CompilerParams(dimension_semantics=sem, vmem_limit_bytes=VMEM_LIMIT)


def _rms(x, g):
    return x * lax.rsqrt(jnp.mean(x * x, axis=-1, keepdims=True) + NORM_EPS) * g


def _row_tile(n):
    for t in (1024, 512, 256, 128, 64, 32, 16, 8):
        if n % t == 0:
            return t
    raise ValueError(n)


def _norm_matmul_kernel(x_ref, g_ref, w_ref, o_ref, xn_ref):
    @pl.when(pl.program_id(1) == 0)
    def _():
        xn_ref[...] = _rms(x_ref[...], g_ref[...]).astype(BF16)

    o_ref[...] = jnp.dot(xn_ref[...], w_ref[...], preferred_element_type=F32)


def norm_matmul(x, g, w, tn):
    n, d = x.shape
    f = w.shape[1]
    tm = _row_tile(n)
    return pl.pallas_call(
        _norm_matmul_kernel,
        out_shape=jax.ShapeDtypeStruct((n, f), F32),
        grid=(n // tm, f // tn),
        in_specs=[pl.BlockSpec((tm, d), lambda i, j: (i, 0)),
                  pl.BlockSpec((1, d), lambda i, j: (0, 0)),
                  pl.BlockSpec((d, tn), lambda i, j: (0, j))],
        out_specs=pl.BlockSpec((tm, tn), lambda i, j: (i, j)),
        scratch_shapes=[pltpu.VMEM((tm, d), BF16)],
        compiler_params=_params("parallel", "arbitrary"),
        name="norm_matmul",
    )(x, g.reshape(1, d), w)


def _proj_res_kernel(*refs, n_x):
    h_ref = refs[0]
    xs = refs[1:1 + n_x]
    ws = refs[1 + n_x:1 + 2 * n_x]
    o_ref = refs[1 + 2 * n_x]
    acc = h_ref[...]
    for x_ref, w_ref in zip(xs, ws):
        acc = acc + jnp.dot(x_ref[...].astype(BF16), w_ref[...], preferred_element_type=F32)
    o_ref[...] = acc


def proj_residual(h, xs, ws):
    n, d = h.shape
    tm = min(_row_tile(n), 512)
    in_specs = [pl.BlockSpec((tm, d), lambda i: (i, 0))]
    in_specs += [pl.BlockSpec((tm, x.shape[1]), lambda i: (i, 0)) for x in xs]
    in_specs += [pl.BlockSpec(w.shape, lambda i: (0, 0)) for w in ws]
    return pl.pallas_call(
        functools.partial(_proj_res_kernel, n_x=len(xs)),
        out_shape=jax.ShapeDtypeStruct((n, d), F32),
        grid=(n // tm,),
        in_specs=in_specs,
        out_specs=pl.BlockSpec((tm, d), lambda i: (i, 0)),
        compiler_params=_params("parallel"),
        name="proj_residual",
    )(h, *xs, *ws)


def _ffn_kernel(h_ref, g_ref, wg_ref, wu_ref, wd_ref, o_ref, xn_ref):
    @pl.when(pl.program_id(1) == 0)
    def _():
        x = h_ref[...]
        xn_ref[...] = _rms(x, g_ref[...]).astype(BF16)
        o_ref[...] = x

    xn = xn_ref[...]
    a = jnp.dot(xn, wg_ref[...], preferred_element_type=F32)
    b = jnp.dot(xn, wu_ref[...], preferred_element_type=F32)
    t = (a * jax.nn.sigmoid(a)) * b
    o_ref[...] += jnp.dot(t.astype(BF16), wd_ref[...], preferred_element_type=F32)


def ffn_residual(h, g, wg, wu, wd, tf=256):
    n, d = h.shape
    ff = wg.shape[1]
    tm = _row_tile(n)
    return pl.pallas_call(
        _ffn_kernel,
        out_shape=jax.ShapeDtypeStruct((n, d), F32),
        grid=(n // tm, ff // tf),
        in_specs=[pl.BlockSpec((tm, d), lambda i, j: (i, 0)),
                  pl.BlockSpec((1, d), lambda i, j: (0, 0)),
                  pl.BlockSpec((d, tf), lambda i, j: (0, j)),
                  pl.BlockSpec((d, tf), lambda i, j: (0, j)),
                  pl.BlockSpec((tf, d), lambda i, j: (j, 0))],
        out_specs=pl.BlockSpec((tm, d), lambda i, j: (i, 0)),
        scratch_shapes=[pltpu.VMEM((tm, d), BF16)],
        compiler_params=_params("parallel", "arbitrary"),
        name="ffn_residual",
    )(h, g.reshape(1, d), wg, wu, wd)


def _ple_kernel(h_ref, g_ref, p_ref, wp_ref, wgate_ref, fn_ref, o_ref, *, final):
    x = h_ref[...]
    xn = _rms(x, g_ref[...]).astype(BF16)
    gate = jax.nn.sigmoid(jnp.dot(xn, wgate_ref[...], preferred_element_type=F32))
    pp = jnp.dot(p_ref[...].astype(BF16), wp_ref[...], preferred_element_type=F32)
    hn = x + pp * gate
    if final:
        hn = _rms(hn, fn_ref[...])
    o_ref[...] = hn


def ple_residual(h, g, p, wp, wgate, final_g, final):
    n, d = h.shape
    pd = p.shape[1]
    tm = min(_row_tile(n), 512)
    return pl.pallas_call(
        functools.partial(_ple_kernel, final=final),
        out_shape=jax.ShapeDtypeStruct((n, d), F32),
        grid=(n // tm,),
        in_specs=[pl.BlockSpec((tm, d), lambda i: (i, 0)),
                  pl.BlockSpec((1, d), lambda i: (0, 0)),
                  pl.BlockSpec((tm, pd), lambda i: (i, 0)),
                  pl.BlockSpec((pd, d), lambda i: (0, 0)),
                  pl.BlockSpec((d, d), lambda i: (0, 0)),
                  pl.BlockSpec((1, d), lambda i: (0, 0))],
        out_specs=pl.BlockSpec((tm, d), lambda i: (i, 0)),
        compiler_params=_params("parallel"),
        name="ple_residual",
    )(h, g.reshape(1, d), p, wp, wgate, final_g.reshape(1, d))


def _softplus(x):
    return jnp.maximum(x, 0.0) + jnp.log1p(jnp.exp(-jnp.abs(x)))


def _rwkv_kernel(zr_ref, zk_ref, zv_ref, zl_ref, sr_ref, sk_ref, sv_ref, sl_ref,
                 mur_ref, muk_ref, muv_ref, mul_ref, w0_ref, wwa_ref, a0_ref, g2_ref,
                 kkw_ref, kaw_ref, rkw_ref, lnw_ref, lnb_ref, s0_ref,
                 hm_ref, mst_ref, min_ref, tri_ref, bd_ref, fm_ref, ft_ref,
                 out_ref, s_out_ref, lr_ref, lk_ref, lv_ref, ll_ref, *, G, C):
    L = G * HEAD_DIM
    R = G * C
    c = pl.program_id(2)

    @pl.when(c == 0)
    def _():
        lr_ref[...] = sr_ref[0]
        lk_ref[...] = sk_ref[0]
        lv_ref[...] = sv_ref[0]
        ll_ref[...] = sl_ref[0]
        s_out_ref[...] = s0_ref[...]

    def shift_mix(x_ref, last_ref, mu_ref):
        x = x_ref[...]
        row = lax.broadcasted_iota(jnp.int32, x.shape, 0)
        prev = jnp.where(row == 0, last_ref[...], pltpu.roll(x, 1, axis=0))
        last_ref[...] = x[C - 1:C, :]
        return x + mu_ref[...] * (prev - x)

    r = shift_mix(zr_ref, lr_ref, mur_ref)
    k = shift_mix(zk_ref, lk_ref, muk_ref)
    v = shift_mix(zv_ref, lv_ref, muv_ref)
    lo = shift_mix(zl_ref, ll_ref, mul_ref)

    wa_in = lo[:, :W_LORA + A_LORA]
    lane = lax.broadcasted_iota(jnp.int32, wa_in.shape, 1)
    wa_in = jnp.where(lane < W_LORA, jnp.tanh(wa_in), wa_in)
    wa = jnp.dot(wa_in, wwa_ref[0], preferred_element_type=F32)
    logw = -_softplus(-(w0_ref[...] + wa[:, :L])) - 0.5
    ld = -jnp.exp(logw)
    a = jax.nn.sigmoid(a0_ref[...] + wa[:, L:])
    g = jnp.dot(jax.nn.sigmoid(lo[:, W_LORA + A_LORA:]), g2_ref[...], preferred_element_type=F32)

    bd = bd_ref[...]
    kkv = k * kkw_ref[...]
    ss = jnp.dot(kkv * kkv, bd, precision=HI, preferred_element_type=F32)
    kkn = kkv / jnp.maximum(jnp.sqrt(ss), 1e-12)
    k2 = k * (1.0 + (a - 1.0) * kaw_ref[...])
    an = -kkn
    bn = kkn * a
    bonus = jnp.dot(r * k2 * rkw_ref[...], bd, precision=HI, preferred_element_type=F32) * v

    cs = jnp.dot(tri_ref[...], ld, precision=HI, preferred_element_type=F32)
    e_neg = jnp.exp(-cs)
    e_pos = jnp.exp(cs)
    hm = hm_ref[...]

    def stack(x):
        return jnp.concatenate([x] * G, axis=0) * hm

    a_s = stack(an * jnp.exp(cs - ld))
    r_s = stack(r * e_pos)
    b_s = stack(bn * e_neg)
    k_s = stack(k2 * e_neg)
    pc = e_pos[C - 1:C, :]

    lhs = jnp.concatenate([a_s, r_s], axis=0)
    m_b = lax.dot_general(lhs, b_s, NT, precision=HI, preferred_element_type=F32)
    m_k = lax.dot_general(lhs, k_s, NT, precision=HI, preferred_element_type=F32)
    strict = mst_ref[...] > 0.5
    incl = min_ref[...] > 0.5
    a_ab = jnp.where(strict, m_b[:R], 0.0)
    a_ak = jnp.where(strict, m_k[:R], 0.0)
    a_rb = jnp.where(incl, m_b[R:], 0.0)
    a_rk = jnp.where(incl, m_k[R:], 0.0)

    s_c = s_out_ref[0]
    w0s = lax.dot_general(lhs, s_c, NT, precision=HI, preferred_element_type=F32)
    v_s = jnp.dot(stack(v), fm_ref[...], precision=HI, preferred_element_type=F32)

    u = w0s[:R] + jnp.dot(a_ak, v_s, precision=HI, preferred_element_type=F32)
    ap = a_ab
    n_dbl = max(1, math.ceil(math.log2(C)))
    for j in range(n_dbl):
        u = u + jnp.dot(ap, u, precision=HI, preferred_element_type=F32)
        if j + 1 < n_dbl:
            ap = jnp.dot(ap, ap, precision=HI, preferred_element_type=F32)

    y = (w0s[R:] + jnp.dot(a_rb, u, precision=HI, preferred_element_type=F32)
         + jnp.dot(a_rk, v_s, precision=HI, preferred_element_type=F32))
    mu_y = jnp.mean(y, axis=-1, keepdims=True)
    yc = y - mu_y
    yn = yc * lax.rsqrt(jnp.mean(yc * yc, axis=-1, keepdims=True) + RWKV_LN_EPS)

    yw = jnp.dot(yn, ft_ref[...], precision=HI, preferred_element_type=F32) * hm
    y_cl = yw[:C]
    for h in range(1, G):
        y_cl = y_cl + yw[h * C:(h + 1) * C]
    out_ref[...] = (y_cl * lnw_ref[...] + lnb_ref[...] + bonus) * g

    uv = jnp.concatenate([u, v_s], axis=0)
    bk = jnp.concatenate([b_s, k_s], axis=0)
    s_new = (s_c + lax.dot_general(uv, bk, TN, precision=HI, preferred_element_type=F32)) * pc
    s_out_ref[0] = s_new


def _rwkv_tables(G, C):
    L, R = G * HEAD_DIM, G * C
    i = np.arange(R)[:, None]
    j = np.arange(R)[None, :]
    l = np.arange(L)
    same = (i // C) == (j // C)
    hm = ((np.arange(R)[:, None] // C) == (l[None, :] // HEAD_DIM))
    tri = np.arange(C)[:, None] >= np.arange(C)[None, :]
    bd = (l[:, None] // HEAD_DIM) == (l[None, :] // HEAD_DIM)
    fm = (l[:, None] % HEAD_DIM) == np.arange(HEAD_DIM)[None, :]
    f = lambda m: jnp.asarray(m.astype(np.float32))
    return [f(hm), f(same & (i > j)), f(same & (i >= j)), f(tri), f(bd), f(fm), f(fm.T)]


def rwkv_mix(z, shift0, s0_cat, prm, B, T, G, C):
    L = G * HEAD_DIM
    P = RWKV_HEADS // G
    NC = T // C
    nb = RWKV_WIDTH // L
    lb = (3 * RWKV_WIDTH) // 256
    row = lambda b, p, c: b * NC + c
    zspec = lambda off: pl.BlockSpec((C, L), lambda b, p, c: (row(b, p, c), off * nb + p))
    sspec = lambda off: pl.BlockSpec((1, 1, L), lambda b, p, c: (b, 0, off * nb + p))
    vspec = lambda off: pl.BlockSpec((1, L), lambda b, p, c: (0, off * nb + p))
    const = lambda a: pl.BlockSpec(a.shape, lambda b, p, c: (0,) * a.ndim)
    tables = _rwkv_tables(G, C)
    mu = prm["mu"].reshape(1, RWKV_PROJ)
    sh = shift0.reshape(B, 1, RWKV_PROJ)
    in_specs = [zspec(0), zspec(1), zspec(2),
                pl.BlockSpec((C, 256), lambda b, p, c: (row(b, p, c), lb)),
                sspec(0), sspec(1), sspec(2),
                pl.BlockSpec((1, 1, 256), lambda b, p, c: (b, 0, lb)),
                vspec(0), vspec(1), vspec(2),
                pl.BlockSpec((1, 256), lambda b, p, c: (0, lb)),
                vspec(0),
                pl.BlockSpec((1, W_LORA + A_LORA, 2 * L), lambda b, p, c: (p, 0, 0)),
                vspec(0),
                pl.BlockSpec((G_LORA, L), lambda b, p, c: (0, p)),
                vspec(0), vspec(0), vspec(0), vspec(0), vspec(0),
                pl.BlockSpec((1, HEAD_DIM, L), lambda b, p, c: (b, 0, p))]
    in_specs += [const(t) for t in tables]
    out, s_t = pl.pallas_call(
        functools.partial(_rwkv_kernel, G=G, C=C),
        out_shape=(jax.ShapeDtypeStruct((B * T, RWKV_WIDTH), F32),
                   jax.ShapeDtypeStruct((B, HEAD_DIM, RWKV_WIDTH), F32)),
        grid=(B, P, NC),
        in_specs=in_specs,
        out_specs=(pl.BlockSpec((C, L), lambda b, p, c: (row(b, p, c), p)),
                   pl.BlockSpec((1, HEAD_DIM, L), lambda b, p, c: (b, 0, p))),
        scratch_shapes=[pltpu.VMEM((1, L), F32), pltpu.VMEM((1, L), F32),
                        pltpu.VMEM((1, L), F32), pltpu.VMEM((1, 256), F32)],
        compiler_params=_params("parallel", "parallel", "arbitrary"),
        name="rwkv_mix",
    )(z, z, z, z, sh, sh, sh, sh, mu, mu, mu, mu,
      prm["w0"], prm["wwa"][G], prm["a0"], prm["g2"], prm["k_k"], prm["k_a"], prm["r_k"],
      prm["ln_w"], prm["ln_b"], s0_cat, *tables)
    return out, s_t


def _pack_wwa(w2, a2, G):
    L = G * HEAD_DIM
    P = RWKV_HEADS // G
    blocks = []
    for p in range(P):
        w2p = w2[:, p * L:(p + 1) * L]
        a2p = a2[:, p * L:(p + 1) * L]
        top = jnp.concatenate([w2p, jnp.zeros_like(w2p)], axis=1)
        bot = jnp.concatenate([jnp.zeros_like(a2p), a2p], axis=1)
        blocks.append(jnp.concatenate([top, bot], axis=0))
    return jnp.stack(blocks)


def _top_blocks(gate, n_iota, n_valid, n_blocks):
    neg = -jnp.inf
    valid = n_iota < n_valid
    g = jnp.where(valid, gate, neg)
    sel = jnp.zeros(gate.shape, F32)
    for _ in range(MOBA_TOPK):
        m = jnp.max(g, axis=-1, keepdims=True)
        idx = jnp.min(jnp.where(g == m, n_iota, n_blocks), axis=-1, keepdims=True)
        pick = n_iota == idx
        sel = jnp.where(jnp.logical_and(pick, valid), 1.0, sel)
        g = jnp.where(pick, neg, g)
    return sel


def _moba_prompt_kernel(q_ref, k_ref, v_ref, o_ref, kbf_ref, vbf_ref, kmean_ref, *, T):
    NB = T // MOBA_BLOCK
    QC = MOBA_QCHUNK
    c = pl.program_id(2)
    scale = HEAD_DIM ** -0.5

    @pl.when(c == 0)
    def _():
        kf = k_ref[...]
        kbf_ref[...] = kf.astype(BF16)
        vbf_ref[...] = v_ref[...].astype(BF16)
        kmean_ref[...] = jnp.sum(kf.reshape(NB, MOBA_BLOCK, LANE), axis=1) * (1.0 / MOBA_BLOCK)

    blk = (c * QC) // MOBA_BLOCK
    q = q_ref[...]
    lane = lax.broadcasted_iota(jnp.int32, (QC, LANE), 1)
    n_iota = lax.broadcasted_iota(jnp.int32, (QC, NB), 1)
    qpos = c * QC + lax.broadcasted_iota(jnp.int32, (QC, MOBA_BLOCK), 0)
    kpos = blk * MOBA_BLOCK + lax.broadcasted_iota(jnp.int32, (QC, MOBA_BLOCK), 1)
    kmean = kmean_ref[...]
    out = jnp.zeros((QC, LANE), F32)
    for hh in range(LANE // HEAD_DIM):
        in_head = (lane // HEAD_DIM) == hh
        qm = jnp.where(in_head, q, 0.0)
        gate = lax.dot_general(qm, kmean, NT, precision=HI, preferred_element_type=F32)
        sel = _top_blocks(gate, n_iota, blk, NB)
        qb = qm.astype(BF16)

        start = pl.multiple_of(blk * MOBA_BLOCK, MOBA_BLOCK)
        s = lax.dot_general(qb, kbf_ref[pl.ds(start, MOBA_BLOCK), :], NT,
                            preferred_element_type=F32) * scale
        s = jnp.where(kpos <= qpos, s, -jnp.inf)
        m = jnp.max(s, axis=-1, keepdims=True)
        p = jnp.exp(s - m)
        l = jnp.sum(p, axis=-1, keepdims=True)
        acc = jnp.dot(p.astype(BF16), vbf_ref[pl.ds(start, MOBA_BLOCK), :], preferred_element_type=F32)

        def body(n, carry):
            m, l, acc = carry
            st = pl.multiple_of(n * MOBA_BLOCK, MOBA_BLOCK)
            s = lax.dot_general(qb, kbf_ref[pl.ds(st, MOBA_BLOCK), :], NT,
                                preferred_element_type=F32) * scale
            picked = jnp.max(jnp.where(n_iota == n, sel, 0.0), axis=-1, keepdims=True) > 0.5
            s = jnp.where(picked, s, -jnp.inf)
            m_new = jnp.maximum(m, jnp.max(s, axis=-1, keepdims=True))
            alpha = jnp.exp(m - m_new)
            p = jnp.exp(s - m_new)
            l = alpha * l + jnp.sum(p, axis=-1, keepdims=True)
            acc = alpha * acc + jnp.dot(p.astype(BF16), vbf_ref[pl.ds(st, MOBA_BLOCK), :],
                                        preferred_element_type=F32)
            return m_new, l, acc

        m, l, acc = lax.fori_loop(0, blk, body, (m, l, acc))
        out = jnp.where(in_head, acc / l, out)
    o_ref[...] = out


def moba_prompt(z, B, T):
    assert T % MOBA_BLOCK == 0 and MOBA_BLOCK % MOBA_QCHUNK == 0
    NQ = T // MOBA_QCHUNK
    P = MOBA_WIDTH // LANE
    q0 = RWKV_PROJ // LANE
    k0 = (RWKV_PROJ + MOBA_WIDTH) // LANE
    v0 = (RWKV_PROJ + 2 * MOBA_WIDTH) // LANE
    return pl.pallas_call(
        functools.partial(_moba_prompt_kernel, T=T),
        out_shape=jax.ShapeDtypeStruct((B * T, MOBA_WIDTH), F32),
        grid=(B, P, NQ),
        in_specs=[pl.BlockSpec((MOBA_QCHUNK, LANE), lambda b, p, c: (b * NQ + c, q0 + p)),
                  pl.BlockSpec((T, LANE), lambda b, p, c: (b, k0 + p)),
                  pl.BlockSpec((T, LANE), lambda b, p, c: (b, v0 + p))],
        out_specs=pl.BlockSpec((MOBA_QCHUNK, LANE), lambda b, p, c: (b * NQ + c, p)),
        scratch_shapes=[pltpu.VMEM((T, LANE), BF16), pltpu.VMEM((T, LANE), BF16),
                        pltpu.VMEM((T // MOBA_BLOCK, LANE), F32)],
        compiler_params=_params("parallel", "parallel", "arbitrary"),
        name="moba_prompt",
    )(z, z, z)


def _moba_sample_kernel(pt_ref, q_ref, kn_ref, vn_ref, *refs, n_pages, tq):
    kp = refs[:n_pages]
    vp = refs[n_pages:2 * n_pages]
    hm_ref = refs[2 * n_pages]
    o_ref = refs[2 * n_pages + 1]
    H = MOBA_HEADS
    R = H * tq
    ppb = MOBA_BLOCK // kp[0].shape[1]
    n_past = n_pages // ppb
    scale = HEAD_DIM ** -0.5
    hm = hm_ref[...]
    qs = jnp.concatenate([q_ref[...]] * H, axis=0) * hm

    means = []
    for n in range(n_past):
        tot = jnp.sum(kp[n * ppb][0], axis=0, keepdims=True)
        for j in range(1, ppb):
            tot = tot + jnp.sum(kp[n * ppb + j][0], axis=0, keepdims=True)
        means.append(tot * (1.0 / MOBA_BLOCK))
    kmean = jnp.concatenate(means, axis=0)
    gate = lax.dot_general(qs, kmean, NT, precision=HI, preferred_element_type=F32)
    n_iota = lax.broadcasted_iota(jnp.int32, (R, n_past), 1)
    sel = _top_blocks(gate, n_iota, n_past, n_past + 1)

    qb = qs.astype(BF16)
    scores = []
    for j in range(n_pages):
        s = lax.dot_general(qb, kp[j][0].astype(BF16), NT, preferred_element_type=F32) * scale
        scores.append(jnp.where(sel[:, j // ppb:j // ppb + 1] > 0.5, s, -jnp.inf))
    s_own = lax.dot_general(qb, kn_ref[...].astype(BF16), NT, preferred_element_type=F32) * scale
    qi = lax.broadcasted_iota(jnp.int32, (R, tq), 0) % tq
    ki = lax.broadcasted_iota(jnp.int32, (R, tq), 1)
    s_own = jnp.where(ki <= qi, s_own, -jnp.inf)

    m = jnp.max(s_own, axis=-1, keepdims=True)
    for s in scores:
        m = jnp.maximum(m, jnp.max(s, axis=-1, keepdims=True))
    p_own = jnp.exp(s_own - m)
    l = jnp.sum(p_own, axis=-1, keepdims=True)
    acc = jnp.dot(p_own.astype(BF16), vn_ref[...].astype(BF16), preferred_element_type=F32)
    for j in range(n_pages):
        p = jnp.exp(scores[j] - m)
        l = l + jnp.sum(p, axis=-1, keepdims=True)
        acc = acc + jnp.dot(p.astype(BF16), vp[j][0].astype(BF16), preferred_element_type=F32)
    om = (acc / l) * hm
    out = om[:tq]
    for h in range(1, H):
        out = out + om[h * tq:(h + 1) * tq]
    o_ref[...] = out


def moba_sample(zq, zk, zv, pool_k, pool_v, page_table, B, T):
    n_pages = page_table.shape[1]
    n_pool, page_size = pool_k.shape[0], pool_k.shape[1]
    assert MOBA_BLOCK % page_size == 0 and (n_pages * page_size) % MOBA_BLOCK == 0
    assert T <= MOBA_BLOCK and T % MOBA_QCHUNK != 0 and n_pages * page_size // MOBA_BLOCK >= MOBA_TOPK
    pk = pool_k.reshape(n_pool, page_size, MOBA_WIDTH)
    pv = pool_v.reshape(n_pool, page_size, MOBA_WIDTH)
    R = MOBA_HEADS * T
    hm = jnp.asarray(((np.arange(R)[:, None] // T) == (np.arange(MOBA_WIDTH)[None, :] // HEAD_DIM))
                     .astype(np.float32))
    rows = pl.BlockSpec((T, MOBA_WIDTH), lambda b, pt: (b, 0))
    page = lambda j: pl.BlockSpec((1, page_size, MOBA_WIDTH), lambda b, pt: (pt[b, j], 0, 0))
    return pl.pallas_call(
        functools.partial(_moba_sample_kernel, n_pages=n_pages, tq=T),
        out_shape=jax.ShapeDtypeStruct((B * T, MOBA_WIDTH), F32),
        grid_spec=pltpu.PrefetchScalarGridSpec(
            num_scalar_prefetch=1,
            grid=(B,),
            in_specs=[rows, rows, rows] + [page(j) for j in range(n_pages)] * 2
            + [pl.BlockSpec(hm.shape, lambda b, pt: (0, 0))],
            out_specs=rows),
        compiler_params=_params("parallel"),
        name="moba_sample",
    )(page_table, zq, zk, zv, *([pk] * n_pages), *([pv] * n_pages), hm)


def _retention_kernel(q_ref, k_ref, v_ref, g_ref, cos_ref, sin_ref, lg_ref, gnw_ref, s0_ref,
                      o_ref, s_out_ref, *, C):
    c = pl.program_id(2)

    @pl.when(c == 0)
    def _():
        s_out_ref[...] = s0_ref[...]

    half = RET_DK // 2
    cos = cos_ref[...]
    sin = sin_ref[...]

    def rot(x):
        x1, x2 = x[:, :half], x[:, half:]
        return jnp.concatenate([x1 * cos - x2 * sin, x1 * sin + x2 * cos], axis=-1)

    qr = rot(q_ref[...])
    kr = rot(k_ref[...]) * (RET_DK ** -0.5)
    v = v_ref[...]
    lg = lg_ref[0]
    ii = lax.broadcasted_iota(jnp.int32, (C, C), 0)
    jj = lax.broadcasted_iota(jnp.int32, (C, C), 1)
    diff = (ii - jj).astype(F32)
    dmask = jnp.where(ii >= jj, jnp.exp(jnp.maximum(diff, 0.0) * lg[:, :C]), 0.0)
    it = lax.broadcasted_iota(jnp.int32, (C, LANE), 0).astype(F32)
    cross = jnp.exp((it + 1.0) * lg)[:, :1]
    kdec = jnp.exp((C - 1.0 - it) * lg)[:, :1]
    sdec = jnp.exp(C * lg)[:, :1]

    s = s_out_ref[0, 0]
    qb = qr.astype(BF16)
    vb = v.astype(BF16)
    att = lax.dot_general(qb, kr.astype(BF16), NT, preferred_element_type=F32) * dmask
    y = (jnp.dot(att.astype(BF16), vb, preferred_element_type=F32)
         + jnp.dot(qb, s.astype(BF16), preferred_element_type=F32) * cross)
    s_out_ref[0, 0] = s * sdec + lax.dot_general(kr * kdec, v, TN, preferred_element_type=F32)

    mu = jnp.mean(y, axis=-1, keepdims=True)
    yc = y - mu
    yn = yc * lax.rsqrt(jnp.mean(yc * yc, axis=-1, keepdims=True) + GN_EPS) * gnw_ref[...]
    g = g_ref[...]
    o_ref[...] = (g * jax.nn.sigmoid(g)) * yn


def retention_mix(z, s0, gn_w, B, T, pos0):
    C = RET_CHUNK if T % RET_CHUNK == 0 else T
    NC = T // C
    half = RET_DK // 2
    pos = (pos0 + jnp.arange(T)).astype(F32)
    inv = ROPE_BASE ** (-jnp.arange(half, dtype=F32) / half)
    ang = pos[:, None] * inv[None, :]
    log_g = jnp.log1p(-jnp.exp2(-5.0 - jnp.arange(RET_HEADS, dtype=F32)))
    lg = jnp.broadcast_to(log_g[:, None, None], (RET_HEADS, 1, LANE))
    kb = RET_QK // RET_DK
    vb = 2 * RET_QK // RET_DV
    gb = (2 * RET_QK + RET_V) // RET_DV
    row = lambda b, h, c: b * NC + c
    return pl.pallas_call(
        functools.partial(_retention_kernel, C=C),
        out_shape=(jax.ShapeDtypeStruct((B * T, RET_V), F32),
                   jax.ShapeDtypeStruct((B, RET_HEADS, RET_DK, RET_DV), F32)),
        grid=(B, RET_HEADS, NC),
        in_specs=[pl.BlockSpec((C, RET_DK), lambda b, h, c: (row(b, h, c), h)),
                  pl.BlockSpec((C, RET_DK), lambda b, h, c: (row(b, h, c), kb + h)),
                  pl.BlockSpec((C, RET_DV), lambda b, h, c: (row(b, h, c), vb + h)),
                  pl.BlockSpec((C, RET_DV), lambda b, h, c: (row(b, h, c), gb + h)),
                  pl.BlockSpec((C, half), lambda b, h, c: (c, 0)),
                  pl.BlockSpec((C, half), lambda b, h, c: (c, 0)),
                  pl.BlockSpec((1, 1, LANE), lambda b, h, c: (h, 0, 0)),
                  pl.BlockSpec((1, RET_DV), lambda b, h, c: (0, h)),
                  pl.BlockSpec((1, 1, RET_DK, RET_DV), lambda b, h, c: (b, h, 0, 0))],
        out_specs=(pl.BlockSpec((C, RET_DV), lambda b, h, c: (row(b, h, c), h)),
                   pl.BlockSpec((1, 1, RET_DK, RET_DV), lambda b, h, c: (b, h, 0, 0))),
        compiler_params=_params("parallel", "parallel", "arbitrary"),
        name="retention_mix",
    )(z, z, z, z, jnp.cos(ang), jnp.sin(ang), lg, gn_w.reshape(1, RET_V), s0)


def _run_trunk(x, p, B, T, pos0, moba_fn, rwkv_s0, rwkv_shift0, ret_s0, G, C, W):
    h = x
    z = norm_matmul(h, W["norm_mix"][0], W["a_w_in"][0], tn=256)
    s0_cat = rwkv_s0.transpose(0, 2, 1, 3).reshape(B, HEAD_DIM, RWKV_WIDTH)
    rwkv_out, s_cat = rwkv_mix(z, rwkv_shift0, s0_cat, W["rwkv"], B, T, G, C)
    rwkv_state = s_cat.reshape(B, HEAD_DIM, RWKV_HEADS, HEAD_DIM).transpose(0, 2, 1, 3)
    zk = z[:, RWKV_PROJ + MOBA_WIDTH:RWKV_PROJ + 2 * MOBA_WIDTH]
    zv = z[:, RWKV_PROJ + 2 * MOBA_WIDTH:]
    moba_out = moba_fn(z, zk, zv)
    shift_t = z.reshape(B, T, A_IN)[:, T - 1, :RWKV_PROJ]
    wo = W["a_w_out"][0]
    h = proj_residual(h, [rwkv_out, moba_out], [wo[:RWKV_WIDTH], wo[RWKV_WIDTH:]])
    h = ffn_residual(h, W["norm_ffn"][0], W["ffn_w_gate"][0], W["ffn_w_up"][0], W["ffn_w_down"][0])
    h = ple_residual(h, W["ple_norm"][0], p[0], W["ple_proj"][0], W["ple_gate"][0],
                     W["final_norm"], final=False)
    z2 = norm_matmul(h, W["norm_mix"][1], W["c_w_in"][0], tn=512)
    gated, ret_state = retention_mix(z2, ret_s0, W["ret_gn_w"][0], B, T, pos0)
    h = proj_residual(h, [gated], [W["c_w_out"][0]])
    h = ffn_residual(h, W["norm_ffn"][1], W["ffn_w_gate"][1], W["ffn_w_up"][1], W["ffn_w_down"][1])
    y = ple_residual(h, W["ple_norm"][1], p[1], W["ple_proj"][1], W["ple_gate"][1],
                     W["final_norm"], final=True)
    k_rows = zk.reshape(1, B, T, MOBA_HEADS, HEAD_DIM)
    v_rows = zv.reshape(1, B, T, MOBA_HEADS, HEAD_DIM)
    return (y.reshape(B, T, D_MODEL), k_rows, v_rows, rwkv_state[None],
            shift_t[None], ret_state[None])


def kernel(x_prompt, x_sample, cache_moba_k, cache_moba_v, state_rwkv, state_rwkv_shift, state_ret, page_table, p_prompt, p_sample, norm_mix, norm_ffn, ffn_w_gate, ffn_w_up, ffn_w_down, ple_norm, ple_gate, ple_proj, a_w_in, rwkv_mu, rwkv_w0, rwkv_w2, rwkv_a0, rwkv_a2, rwkv_g2, rwkv_k_k, rwkv_k_a, rwkv_r_k, rwkv_ln_w, rwkv_ln_b, a_w_out, c_w_in, ret_gn_w, c_w_out, final_norm):
    assert norm_mix.shape[0] == DEPTH == 2
    B, T, _ = x_prompt.shape
    Bd, Td, _ = x_sample.shape
    bf = lambda w: w.astype(BF16)
    row = lambda a: a.reshape(1, RWKV_WIDTH)
    g_prompt = 2
    g_sample = RWKV_HEADS
    rwkv = dict(mu=rwkv_mu[0], w0=row(rwkv_w0[0]), a0=row(rwkv_a0[0]), g2=rwkv_g2[0],
                k_k=row(rwkv_k_k[0]), k_a=row(rwkv_k_a[0]), r_k=row(rwkv_r_k[0]),
                ln_w=row(rwkv_ln_w[0]), ln_b=row(rwkv_ln_b[0]),
                wwa={g: _pack_wwa(rwkv_w2[0], rwkv_a2[0], g) for g in (g_prompt, g_sample)})
    W = dict(norm_mix=norm_mix, norm_ffn=norm_ffn, ffn_w_gate=bf(ffn_w_gate), ffn_w_up=bf(ffn_w_up),
             ffn_w_down=bf(ffn_w_down), ple_norm=ple_norm, ple_gate=bf(ple_gate), ple_proj=bf(ple_proj),
             a_w_in=bf(a_w_in), a_w_out=bf(a_w_out), c_w_in=bf(c_w_in), c_w_out=bf(c_w_out),
             ret_gn_w=ret_gn_w, final_norm=final_norm, rwkv=rwkv)

    yp, kp, vp, rsp, shp, rtp = _run_trunk(
        x_prompt.reshape(B * T, D_MODEL), p_prompt.reshape(DEPTH, B * T, PLE_DIM), B, T, 0,
        lambda z, zk, zv: moba_prompt(z, B, T),
        jnp.zeros((B, RWKV_HEADS, HEAD_DIM, HEAD_DIM), F32), jnp.zeros((B, RWKV_PROJ), F32),
        jnp.zeros((B, RET_HEADS, RET_DK, RET_DV), F32), g_prompt, min(RWKV_CHUNK, T), W)

    past_len = page_table.shape[1] * cache_moba_k.shape[2]
    q_lo = RWKV_PROJ
    ys, ks, vs, rss, shs, rts = _run_trunk(
        x_sample.reshape(Bd * Td, D_MODEL), p_sample.reshape(DEPTH, Bd * Td, PLE_DIM), Bd, Td, past_len,
        lambda z, zk, zv: moba_sample(z[:, q_lo:q_lo + MOBA_WIDTH], zk, zv, cache_moba_k[0],
                                      cache_moba_v[0], page_table, Bd, Td),
        state_rwkv[0], state_rwkv_shift[0], state_ret[0], g_sample, Td, W)

    return (yp, ys, kp, vp, ks, vs, rsp, rss, shp, shs, rtp, rts)
```

```python
import functools
import math

import numpy as np
import jax
import jax.numpy as jnp
from jax import lax
from jax.experimental import pallas as pl
from jax.experimental.pallas import tpu as pltpu

F32 = jnp.float32
BF16 = jnp.bfloat16
HI = lax.Precision.HIGHEST

D_MODEL = 1024
DEPTH = 2
PLE_DIM = 256
HEAD_DIM = 64
RWKV_HEADS = 8
RWKV_WIDTH = RWKV_HEADS * HEAD_DIM
W_LORA = 64
A_LORA = 64
G_LORA = 128
RWKV_PROJ = 3 * RWKV_WIDTH + W_LORA + A_LORA + G_LORA
RWKV_LN_EPS = 64e-5
RWKV_CHUNK = 64
MOBA_HEADS = 8
MOBA_WIDTH = MOBA_HEADS * HEAD_DIM
MOBA_BLOCK = 256
MOBA_TOPK = 3
MOBA_QCHUNK = 128
MOBA_KEY_GROUP = 4 * MOBA_BLOCK
A_IN = RWKV_PROJ + 3 * MOBA_WIDTH
RET_HEADS = 4
RET_DK = 256
RET_DV = 512
RET_QK = RET_HEADS * RET_DK
RET_V = RET_HEADS * RET_DV
C_IN = 2 * RET_QK + 2 * RET_V
RET_CHUNK = 128
ROPE_BASE = 10000.0
D_FF = 2816
NORM_EPS = 1e-6
GN_EPS = 1e-5

LANE = 128
VMEM_LIMIT = 56 * 1024 * 1024

NN = (((1,), (0,)), ((), ()))
NT = (((1,), (1,)), ((), ()))
TN = (((0,), (0,)), ((), ()))


def _params(*sem):
    return pltpu.CompilerParams(dimension_semantics=sem, vmem_limit_bytes=VMEM_LIMIT)


def _rms(x, g):
    return x * lax.rsqrt(jnp.mean(x * x, axis=-1, keepdims=True) + NORM_EPS) * g


def _row_tile(n):
    for t in (1024, 512, 256, 128, 64, 32, 16, 8):
        if n % t == 0:
            return t
    raise ValueError(n)


def _norm_matmul_kernel(x_ref, g_ref, w_ref, o_ref, xn_ref):
    @pl.when(pl.program_id(1) == 0)
    def _():
        xn_ref[...] = _rms(x_ref[...], g_ref[...]).astype(BF16)

    o_ref[...] = jnp.dot(xn_ref[...], w_ref[...], preferred_element_type=F32)


def norm_matmul(x, g, w, tn):
    n, d = x.shape
    f = w.shape[1]
    tm = _row_tile(n)
    return pl.pallas_call(
        _norm_matmul_kernel,
        out_shape=jax.ShapeDtypeStruct((n, f), F32),
        grid=(n // tm, f // tn),
        in_specs=[pl.BlockSpec((tm, d), lambda i, j: (i, 0)),
                  pl.BlockSpec((1, d), lambda i, j: (0, 0)),
                  pl.BlockSpec((d, tn), lambda i, j: (0, j))],
        out_specs=pl.BlockSpec((tm, tn), lambda i, j: (i, j)),
        scratch_shapes=[pltpu.VMEM((tm, d), BF16)],
        compiler_params=_params("parallel", "arbitrary"),
        name="norm_matmul",
    )(x, g.reshape(1, d), w)


def _proj_res_kernel(*refs, n_x):
    h_ref = refs[0]
    xs = refs[1:1 + n_x]
    ws = refs[1 + n_x:1 + 2 * n_x]
    o_ref = refs[1 + 2 * n_x]
    acc = h_ref[...]
    for x_ref, w_ref in zip(xs, ws):
        acc = acc + jnp.dot(x_ref[...].astype(BF16), w_ref[...], preferred_element_type=F32)
    o_ref[...] = acc


def proj_residual(h, xs, ws):
    n, d = h.shape
    tm = min(_row_tile(n), 512)
    in_specs = [pl.BlockSpec((tm, d), lambda i: (i, 0))]
    in_specs += [pl.BlockSpec((tm, x.shape[1]), lambda i: (i, 0)) for x in xs]
    in_specs += [pl.BlockSpec(w.shape, lambda i: (0, 0)) for w in ws]
    return pl.pallas_call(
        functools.partial(_proj_res_kernel, n_x=len(xs)),
        out_shape=jax.ShapeDtypeStruct((n, d), F32),
        grid=(n // tm,),
        in_specs=in_specs,
        out_specs=pl.BlockSpec((tm, d), lambda i: (i, 0)),
        compiler_params=_params("parallel"),
        name="proj_residual",
    )(h, *xs, *ws)


def _ffn_kernel(h_ref, g_ref, wg_ref, wu_ref, wd_ref, o_ref, xn_ref):
    @pl.when(pl.program_id(1) == 0)
    def _():
        x = h_ref[...]
        xn_ref[...] = _rms(x, g_ref[...]).astype(BF16)
        o_ref[...] = x

    xn = xn_ref[...]
    a = jnp.dot(xn, wg_ref[...], preferred_element_type=F32)
    b = jnp.dot(xn, wu_ref[...], preferred_element_type=F32)
    t = (a * jax.nn.sigmoid(a)) * b
    o_ref[...] += jnp.dot(t.astype(BF16), wd_ref[...], preferred_element_type=F32)


def ffn_residual(h, g, wg, wu, wd, tf=256):
    n, d = h.shape
    ff = wg.shape[1]
    tm = _row_tile(n)
    return pl.pallas_call(
        _ffn_kernel,
        out_shape=jax.ShapeDtypeStruct((n, d), F32),
        grid=(n // tm, ff // tf),
        in_specs=[pl.BlockSpec((tm, d), lambda i, j: (i, 0)),
                  pl.BlockSpec((1, d), lambda i, j: (0, 0)),
                  pl.BlockSpec((d, tf), lambda i, j: (0, j)),
                  pl.BlockSpec((d, tf), lambda i, j: (0, j)),
                  pl.BlockSpec((tf, d), lambda i, j: (j, 0))],
        out_specs=pl.BlockSpec((tm, d), lambda i, j: (i, 0)),
        scratch_shapes=[pltpu.VMEM((tm, d), BF16)],
        compiler_params=_params("parallel", "arbitrary"),
        name="ffn_residual",
    )(h, g.reshape(1, d), wg, wu, wd)


def _ple_kernel(h_ref, g_ref, p_ref, wp_ref, wgate_ref, fn_ref, o_ref, *, final):
    x = h_ref[...]
    xn = _rms(x, g_ref[...]).astype(BF16)
    gate = jax.nn.sigmoid(jnp.dot(xn, wgate_ref[...], preferred_element_type=F32))
    pp = jnp.dot(p_ref[...].astype(BF16), wp_ref[...], preferred_element_type=F32)
    hn = x + pp * gate
    if final:
        hn = _rms(hn, fn_ref[...])
    o_ref[...] = hn


def ple_residual(h, g, p, wp, wgate, final_g, final):
    n, d = h.shape
    pd = p.shape[1]
    tm = min(_row_tile(n), 512)
    return pl.pallas_call(
        functools.partial(_ple_kernel, final=final),
        out_shape=jax.ShapeDtypeStruct((n, d), F32),
        grid=(n // tm,),
        in_specs=[pl.BlockSpec((tm, d), lambda i: (i, 0)),
                  pl.BlockSpec((1, d), lambda i: (0, 0)),
                  pl.BlockSpec((tm, pd), lambda i: (i, 0)),
                  pl.BlockSpec((pd, d), lambda i: (0, 0)),
                  pl.BlockSpec((d, d), lambda i: (0, 0)),
                  pl.BlockSpec((1, d), lambda i: (0, 0))],
        out_specs=pl.BlockSpec((tm, d), lambda i: (i, 0)),
        compiler_params=_params("parallel"),
        name="ple_residual",
    )(h, g.reshape(1, d), p, wp, wgate, final_g.reshape(1, d))


def _softplus(x):
    return jnp.maximum(x, 0.0) + jnp.log1p(jnp.exp(-jnp.abs(x)))


def _split2(x):
    hi = x.astype(BF16)
    return hi, (x - hi.astype(F32)).astype(BF16)


def _split3(x):
    hi = x.astype(BF16)
    r1 = x - hi.astype(F32)
    mid = r1.astype(BF16)
    return hi, mid, (r1 - mid.astype(F32)).astype(BF16)


def _dot3(a, b, dims=NN):
    ah, al = _split2(a)
    bh, bl = _split2(b)
    d = lambda x, y: lax.dot_general(x, y, dims, preferred_element_type=F32)
    return d(ah, bh) + (d(ah, bl) + d(al, bh))


def _dot_mask(a, m, mask_left=False):
    d = lambda x: lax.dot_general(m, x, NN, preferred_element_type=F32) if mask_left else \
        lax.dot_general(x, m, NN, preferred_element_type=F32)
    hi, mid, lo = _split3(a)
    return d(hi) + (d(mid) + d(lo))


def _rwkv_kernel(zr_ref, zk_ref, zv_ref, zl_ref, sr_ref, sk_ref, sv_ref, sl_ref,
                 mur_ref, muk_ref, muv_ref, mul_ref, w0_ref, wwa_ref, a0_ref, g2_ref,
                 kkw_ref, kaw_ref, rkw_ref, lnw_ref, lnb_ref, s0_ref,
                 hm_ref, mst_ref, min_ref, tri_ref, bd_ref,
                 out_ref, s_out_ref, lr_ref, lk_ref, lv_ref, ll_ref, sbd_ref, *, G, C):
    L = G * HEAD_DIM
    R = G * C
    NG = RWKV_HEADS // G
    c = pl.program_id(1)
    wide = R % LANE == 0

    @pl.when(c == 0)
    def _():
        lr_ref[...] = sr_ref[0]
        lk_ref[...] = sk_ref[0]
        lv_ref[...] = sv_ref[0]
        ll_ref[...] = sl_ref[0]
        bdm = bd_ref[:L, :L].astype(F32)
        for gi in range(NG):
            sc = s0_ref[0, :, gi * L:(gi + 1) * L]
            sbd_ref[gi] = jnp.concatenate([sc] * G, axis=0) * bdm

    def shift_mix(x_ref, last_ref, mu_ref):
        x = x_ref[...]
        row = lax.broadcasted_iota(jnp.int32, x.shape, 0)
        prev = jnp.where(row == 0, last_ref[...], pltpu.roll(x, 1, axis=0))
        last_ref[...] = x[C - 1:C, :]
        return x + mu_ref[...] * (prev - x)

    r = shift_mix(zr_ref, lr_ref, mur_ref)
    k = shift_mix(zk_ref, lk_ref, muk_ref)
    v = shift_mix(zv_ref, lv_ref, muv_ref)
    lo = shift_mix(zl_ref, ll_ref, mul_ref)

    wa_in = lo[:, :W_LORA + A_LORA]
    lane = lax.broadcasted_iota(jnp.int32, wa_in.shape, 1)
    wa_in = jnp.where(lane < W_LORA, jnp.tanh(wa_in), wa_in)
    wa = jnp.dot(wa_in, wwa_ref[...], preferred_element_type=F32)
    logw = -_softplus(-(w0_ref[...] + wa[:, :RWKV_WIDTH])) - 0.5
    ld = -jnp.exp(logw)
    a = jax.nn.sigmoid(a0_ref[...] + wa[:, RWKV_WIDTH:])
    g = jnp.dot(jax.nn.sigmoid(lo[:, W_LORA + A_LORA:]), g2_ref[...], preferred_element_type=F32)

    kkv = k * kkw_ref[...]
    k2 = k * (1.0 + (a - 1.0) * kaw_ref[...])
    sums = _dot_mask(jnp.concatenate([kkv * kkv, r * k2 * rkw_ref[...]], axis=0), bd_ref[...])
    kkn = kkv / jnp.maximum(jnp.sqrt(sums[:C]), 1e-12)
    bonus = sums[C:] * v
    an = -kkn
    bn = kkn * a

    cs = _dot_mask(ld, tri_ref[...], mask_left=True)
    e_neg = jnp.exp(-cs)
    e_pos = jnp.exp(cs)
    at = an * jnp.exp(cs - ld)
    rt = r * e_pos
    bt = bn * e_neg
    kt = k2 * e_neg
    pc = e_pos[C - 1:C, :]

    hm = hm_ref[...]
    strict = mst_ref[...] > 0.5
    incl = min_ref[...] > 0.5
    n_dbl = max(1, math.ceil(math.log2(C)))
    ys = []
    for gi in range(NG):
        sl = slice(gi * L, (gi + 1) * L)
        stack = lambda x: jnp.concatenate([x[:, sl]] * G, axis=0) * hm
        a_s, r_s, b_s, k_s, v_s = stack(at), stack(rt), stack(bt), stack(kt), stack(v)
        lhs = jnp.concatenate([a_s, r_s], axis=0)
        bk = jnp.concatenate([b_s, k_s], axis=0)
        if wide:
            m = _dot3(lhs, bk, NT)
            m_b, m_k = m[:, :R], m[:, R:]
        else:
            m_b, m_k = _dot3(lhs, b_s, NT), _dot3(lhs, k_s, NT)
        a_ab = jnp.where(strict, m_b[:R], 0.0)
        a_ak = jnp.where(strict, m_k[:R], 0.0)
        a_rb = jnp.where(incl, m_b[R:], 0.0)
        a_rk = jnp.where(incl, m_k[R:], 0.0)

        sb = sbd_ref[gi]
        w0s = _dot3(lhs, sb, NT)
        u = w0s[:R] + _dot3(a_ak, v_s)
        ap = a_ab
        for j in range(n_dbl):
            if j + 1 == n_dbl:
                u = u + _dot3(ap, u)
            elif wide:
                t = _dot3(ap, jnp.concatenate([u, ap], axis=1))
                u, ap = u + t[:, :L], t[:, L:]
            else:
                u, ap = u + _dot3(ap, u), _dot3(ap, ap)
        if wide:
            y = w0s[R:] + _dot3(jnp.concatenate([a_rb, a_rk], axis=1), jnp.concatenate([u, v_s], axis=0))
        else:
            y = w0s[R:] + _dot3(a_rb, u) + _dot3(a_rk, v_s)

        mu_y = jnp.sum(y, axis=-1, keepdims=True) * (1.0 / HEAD_DIM)
        yc = (y - mu_y) * hm
        yn = yc * lax.rsqrt(jnp.sum(yc * yc, axis=-1, keepdims=True) * (1.0 / HEAD_DIM) + RWKV_LN_EPS)
        y_cl = yn[:C]
        for h in range(1, G):
            y_cl = y_cl + yn[h * C:(h + 1) * C]
        ys.append(y_cl)

        uv = jnp.concatenate([u, v_s], axis=0)
        s_new = (sb + _dot3(uv, bk, TN)) * pc[:, sl]
        sbd_ref[gi] = s_new
        s_cat = s_new[:HEAD_DIM]
        for h in range(1, G):
            s_cat = s_cat + s_new[h * HEAD_DIM:(h + 1) * HEAD_DIM]
        s_out_ref[0, :, sl] = s_cat
    y_all = ys[0] if NG == 1 else jnp.concatenate(ys, axis=1)
    out_ref[...] = (y_all * lnw_ref[...] + lnb_ref[...] + bonus) * g


def _rwkv_tables(G, C):
    L, R = G * HEAD_DIM, G * C
    i = np.arange(R)[:, None]
    j = np.arange(R)[None, :]
    l = np.arange(RWKV_WIDTH)
    same = (i // C) == (j // C)
    hm = ((np.arange(R)[:, None] // C) == (np.arange(L)[None, :] // HEAD_DIM))
    tri = np.arange(C)[:, None] >= np.arange(C)[None, :]
    bd = (l[:, None] // HEAD_DIM) == (l[None, :] // HEAD_DIM)
    f = lambda m, dt=np.float32: jnp.asarray(m.astype(np.float32), dtype=dt)
    return [f(hm), f(same & (i > j)), f(same & (i >= j)), f(tri, BF16), f(bd, BF16)]


def rwkv_mix(z, shift0, s0_cat, prm, B, T, G, C):
    L = G * HEAD_DIM
    NG = RWKV_HEADS // G
    NC = T // C
    W = RWKV_WIDTH
    lw = W_LORA + A_LORA + G_LORA
    lb = (3 * W) // lw
    row = lambda b, c: b * NC + c
    zspec = lambda off: pl.BlockSpec((C, W), lambda b, c: (row(b, c), off))
    sspec = lambda off: pl.BlockSpec((1, 1, W), lambda b, c: (b, 0, off))
    vspec = lambda off: pl.BlockSpec((1, W), lambda b, c: (0, off))
    const = lambda a: pl.BlockSpec(a.shape, lambda b, c: (0,) * a.ndim)
    tables = _rwkv_tables(G, C)
    mu = prm["mu"].reshape(1, RWKV_PROJ)
    sh = shift0.reshape(B, 1, RWKV_PROJ)
    in_specs = [zspec(0), zspec(1), zspec(2),
                pl.BlockSpec((C, lw), lambda b, c: (row(b, c), lb)),
                sspec(0), sspec(1), sspec(2),
                pl.BlockSpec((1, 1, lw), lambda b, c: (b, 0, lb)),
                vspec(0), vspec(1), vspec(2),
                pl.BlockSpec((1, lw), lambda b, c: (0, lb)),
                vspec(0),
                const(prm["wwa"]),
                vspec(0),
                const(prm["g2"]),
                vspec(0), vspec(0), vspec(0), vspec(0), vspec(0),
                pl.BlockSpec((1, HEAD_DIM, W), lambda b, c: (b, 0, 0))]
    in_specs += [const(t) for t in tables]
    out, s_t = pl.pallas_call(
        functools.partial(_rwkv_kernel, G=G, C=C),
        out_shape=(jax.ShapeDtypeStruct((B * T, W), F32),
                   jax.ShapeDtypeStruct((B, HEAD_DIM, W), F32)),
        grid=(B, NC),
        in_specs=in_specs,
        out_specs=(pl.BlockSpec((C, W), lambda b, c: (row(b, c), 0)),
                   pl.BlockSpec((1, HEAD_DIM, W), lambda b, c: (b, 0, 0))),
        scratch_shapes=[pltpu.VMEM((1, W), F32), pltpu.VMEM((1, W), F32),
                        pltpu.VMEM((1, W), F32), pltpu.VMEM((1, lw), F32),
                        pltpu.VMEM((NG, L, L), F32)],
        compiler_params=_params("parallel", "arbitrary"),
        name="rwkv_mix",
    )(z, z, z, z, sh, sh, sh, sh, mu, mu, mu, mu,
      prm["w0"], prm["wwa"], prm["a0"], prm["g2"], prm["k_k"], prm["k_a"], prm["r_k"],
      prm["ln_w"], prm["ln_b"], s0_cat, *tables)
    return out, s_t


def _pack_wwa(w2, a2):
    top = jnp.concatenate([w2, jnp.zeros_like(w2)], axis=1)
    bot = jnp.concatenate([jnp.zeros_like(a2), a2], axis=1)
    return jnp.concatenate([top, bot], axis=0)


def _top_blocks(gate, n_iota, n_valid, n_blocks, axis=-1):
    neg = -jnp.inf
    valid = n_iota < n_valid
    g = jnp.where(valid, gate, neg)
    sel = jnp.zeros(gate.shape, F32)
    for _ in range(MOBA_TOPK):
        m = jnp.max(g, axis=axis, keepdims=True)
        idx = jnp.min(jnp.where(g == m, n_iota, n_blocks), axis=axis, keepdims=True)
        pick = n_iota == idx
        sel = jnp.where(jnp.logical_and(pick, valid), 1.0, sel)
        g = jnp.where(pick, neg, g)
    return sel


def _moba_prompt_kernel(q_ref, k_ref, v_ref, e_ref, o_ref, kbf_ref, vbf_ref, kmean_ref, *, T):
    NB = T // MOBA_BLOCK
    QB = MOBA_BLOCK
    KG = MOBA_KEY_GROUP
    NH = LANE // HEAD_DIM
    blk = pl.program_id(2)
    scale = HEAD_DIM ** -0.5

    @pl.when(blk == 0)
    def _():
        kf = k_ref[...]
        kbf_ref[...] = kf.astype(BF16)
        vbf_ref[...] = v_ref[...].astype(BF16)
        kmean_ref[...] = jnp.sum(kf.reshape(NB, MOBA_BLOCK, LANE), axis=1) * (1.0 / MOBA_BLOCK)

    q = q_ref[...]
    lane = lax.broadcasted_iota(jnp.int32, (QB, LANE), 1)
    n_iota = lax.broadcasted_iota(jnp.int32, (NB, QB), 0)
    kmean = kmean_ref[...]
    in_head, qbs, sels = [], [], []
    for hh in range(NH):
        ih = (lane // HEAD_DIM) == hh
        qm = jnp.where(ih, q, 0.0)
        gate = lax.dot_general(kmean, qm, NT, precision=HI, preferred_element_type=F32)
        in_head.append(ih)
        qbs.append(qm.astype(BF16))
        sels.append(_top_blocks(gate, n_iota, blk, NB, axis=0))

    qpos = blk * QB + lax.broadcasted_iota(jnp.int32, (QB, KG), 0)
    kidx = lax.broadcasted_iota(jnp.int32, (QB, KG), 1)
    n_groups = (blk * QB) // KG + 1

    def body(i, carry):
        gi = n_groups - 1 - i
        st = pl.multiple_of(gi * KG, KG)
        kg = kbf_ref[pl.ds(st, KG), :]
        vg = vbf_ref[pl.ds(st, KG), :]
        e = e_ref[gi]
        kpos = gi * KG + kidx
        own = jnp.logical_and(kpos >= blk * QB, kpos <= qpos)
        new = []
        for hh in range(NH):
            m, l, acc = carry[hh]
            s = lax.dot_general(qbs[hh], kg, NT, preferred_element_type=F32) * scale
            picked = lax.dot_general(sels[hh], e, TN, preferred_element_type=F32) > 0.5
            s = jnp.where(jnp.logical_or(picked, own), s, -jnp.inf)
            m_new = jnp.maximum(m, jnp.max(s, axis=-1, keepdims=True))
            alpha = jnp.exp(m - m_new)
            p = jnp.exp(s - m_new)
            l = alpha * l + jnp.sum(p, axis=-1, keepdims=True)
            acc = alpha * acc + jnp.dot(p.astype(BF16), vg, preferred_element_type=F32)
            new.append((m_new, l, acc))
        return tuple(new)

    init = tuple((jnp.full((QB, 1), -jnp.inf, F32), jnp.zeros((QB, 1), F32), jnp.zeros((QB, LANE), F32))
                 for _ in range(NH))
    res = lax.fori_loop(0, n_groups, body, init)
    out = jnp.zeros((QB, LANE), F32)
    for hh in range(NH):
        _, l, acc = res[hh]
        out = jnp.where(in_head[hh], acc / l, out)
    o_ref[...] = out


def moba_prompt(z, B, T):
    assert T % MOBA_KEY_GROUP == 0 and MOBA_BLOCK % MOBA_QCHUNK == 0
    NB = T // MOBA_BLOCK
    NKG = T // MOBA_KEY_GROUP
    P = MOBA_WIDTH // LANE
    q0 = RWKV_PROJ // LANE
    k0 = (RWKV_PROJ + MOBA_WIDTH) // LANE
    v0 = (RWKV_PROJ + 2 * MOBA_WIDTH) // LANE
    key_block = (np.arange(T) // MOBA_BLOCK).reshape(NKG, 1, MOBA_KEY_GROUP)
    e = jnp.asarray((key_block == np.arange(NB)[None, :, None]).astype(np.float32))
    return pl.pallas_call(
        functools.partial(_moba_prompt_kernel, T=T),
        out_shape=jax.ShapeDtypeStruct((B * T, MOBA_WIDTH), F32),
        grid=(B, P, NB),
        in_specs=[pl.BlockSpec((MOBA_BLOCK, LANE), lambda b, p, c: (b * NB + c, q0 + p)),
                  pl.BlockSpec((T, LANE), lambda b, p, c: (b, k0 + p)),
                  pl.BlockSpec((T, LANE), lambda b, p, c: (b, v0 + p)),
                  pl.BlockSpec(e.shape, lambda b, p, c: (0, 0, 0))],
        out_specs=pl.BlockSpec((MOBA_BLOCK, LANE), lambda b, p, c: (b * NB + c, p)),
        scratch_shapes=[pltpu.VMEM((T, LANE), BF16), pltpu.VMEM((T, LANE), BF16),
                        pltpu.VMEM((NB, LANE), F32)],
        compiler_params=_params("parallel", "parallel", "arbitrary"),
        name="moba_prompt",
    )(z, z, z, e)


def _moba_sample_kernel(pt_ref, q_ref, kn_ref, vn_ref, *refs, n_pages, tq):
    kp = refs[:n_pages]
    vp = refs[n_pages:2 * n_pages]
    hm_ref = refs[2 * n_pages]
    o_ref = refs[2 * n_pages + 1]
    H = MOBA_HEADS
    R = H * tq
    ppb = MOBA_BLOCK // kp[0].shape[1]
    n_past = n_pages // ppb
    scale = HEAD_DIM ** -0.5
    hm = hm_ref[...]
    qs = jnp.concatenate([q_ref[...]] * H, axis=0) * hm

    means = []
    for n in range(n_past):
        tot = jnp.sum(kp[n * ppb][0], axis=0, keepdims=True)
        for j in range(1, ppb):
            tot = tot + jnp.sum(kp[n * ppb + j][0], axis=0, keepdims=True)
        means.append(tot * (1.0 / MOBA_BLOCK))
    kmean = jnp.concatenate(means, axis=0)
    gate = lax.dot_general(qs, kmean, NT, precision=HI, preferred_element_type=F32)
    n_iota = lax.broadcasted_iota(jnp.int32, (R, n_past), 1)
    sel = _top_blocks(gate, n_iota, n_past, n_past + 1)

    qb = qs.astype(BF16)
    scores = []
    for j in range(n_pages):
        s = lax.dot_general(qb, kp[j][0].astype(BF16), NT, preferred_element_type=F32) * scale
        scores.append(jnp.where(sel[:, j // ppb:j // ppb + 1] > 0.5, s, -jnp.inf))
    s_own = lax.dot_general(qb, kn_ref[...].astype(BF16), NT, preferred_element_type=F32) * scale
    qi = lax.broadcasted_iota(jnp.int32, (R, tq), 0) % tq
    ki = lax.broadcasted_iota(jnp.int32, (R, tq), 1)
    s_own = jnp.where(ki <= qi, s_own, -jnp.inf)

    m = jnp.max(s_own, axis=-1, keepdims=True)
    for s in scores:
        m = jnp.maximum(m, jnp.max(s, axis=-1, keepdims=True))
    p_own = jnp.exp(s_own - m)
    l = jnp.sum(p_own, axis=-1, keepdims=True)
    acc = jnp.dot(p_own.astype(BF16), vn_ref[...].astype(BF16), preferred_element_type=F32)
    for j in range(n_pages):
        p = jnp.exp(scores[j] - m)
        l = l + jnp.sum(p, axis=-1, keepdims=True)
        acc = acc + jnp.dot(p.astype(BF16), vp[j][0].astype(BF16), preferred_element_type=F32)
    om = (acc / l) * hm
    out = om[:tq]
    for h in range(1, H):
        out = out + om[h * tq:(h + 1) * tq]
    o_ref[...] = out


def moba_sample(zq, zk, zv, pool_k, pool_v, page_table, B, T):
    n_pages = page_table.shape[1]
    n_pool, page_size = pool_k.shape[0], pool_k.shape[1]
    assert MOBA_BLOCK % page_size == 0 and (n_pages * page_size) % MOBA_BLOCK == 0
    assert T <= MOBA_BLOCK and T % MOBA_QCHUNK != 0 and n_pages * page_size // MOBA_BLOCK >= MOBA_TOPK
    pk = pool_k.reshape(n_pool, page_size, MOBA_WIDTH)
    pv = pool_v.reshape(n_pool, page_size, MOBA_WIDTH)
    R = MOBA_HEADS * T
    hm = jnp.asarray(((np.arange(R)[:, None] // T) == (np.arange(MOBA_WIDTH)[None, :] // HEAD_DIM))
                     .astype(np.float32))
    rows = pl.BlockSpec((T, MOBA_WIDTH), lambda b, pt: (b, 0))
    page = lambda j: pl.BlockSpec((1, page_size, MOBA_WIDTH), lambda b, pt: (pt[b, j], 0, 0))
    return pl.pallas_call(
        functools.partial(_moba_sample_kernel, n_pages=n_pages, tq=T),
        out_shape=jax.ShapeDtypeStruct((B * T, MOBA_WIDTH), F32),
        grid_spec=pltpu.PrefetchScalarGridSpec(
            num_scalar_prefetch=1,
            grid=(B,),
            in_specs=[rows, rows, rows] + [page(j) for j in range(n_pages)] * 2
            + [pl.BlockSpec(hm.shape, lambda b, pt: (0, 0))],
            out_specs=rows),
        compiler_params=_params("parallel"),
        name="moba_sample",
    )(page_table, zq, zk, zv, *([pk] * n_pages), *([pv] * n_pages), hm)


def _retention_kernel(q_ref, k_ref, v_ref, g_ref, cos_ref, sin_ref, lg_ref, gnw_ref, s0_ref,
                      o_ref, s_out_ref, *, C):
    c = pl.program_id(2)

    @pl.when(c == 0)
    def _():
        s_out_ref[...] = s0_ref[...]

    half = RET_DK // 2
    cos = cos_ref[...]
    sin = sin_ref[...]

    def rot(x):
        x1, x2 = x[:, :half], x[:, half:]
        return jnp.concatenate([x1 * cos - x2 * sin, x1 * sin + x2 * cos], axis=-1)

    qr = rot(q_ref[...])
    kr = rot(k_ref[...]) * (RET_DK ** -0.5)
    v = v_ref[...]
    lg = lg_ref[0]
    ii = lax.broadcasted_iota(jnp.int32, (C, C), 0)
    jj = lax.broadcasted_iota(jnp.int32, (C, C), 1)
    diff = (ii - jj).astype(F32)
    dmask = jnp.where(ii >= jj, jnp.exp(jnp.maximum(diff, 0.0) * lg[:, :C]), 0.0)
    it = lax.broadcasted_iota(jnp.int32, (C, LANE), 0).astype(F32)
    cross = jnp.exp((it + 1.0) * lg)[:, :1]
    kdec = jnp.exp((C - 1.0 - it) * lg)[:, :1]
    sdec = jnp.exp(C * lg)[:, :1]

    s = s_out_ref[0, 0]
    qb = qr.astype(BF16)
    vb = v.astype(BF16)
    att = lax.dot_general(qb, kr.astype(BF16), NT, preferred_element_type=F32) * dmask
    y = (jnp.dot(att.astype(BF16), vb, preferred_element_type=F32)
         + jnp.dot(qb, s.astype(BF16), preferred_element_type=F32) * cross)
    s_out_ref[0, 0] = s * sdec + lax.dot_general(kr * kdec, v, TN, preferred_element_type=F32)

    mu = jnp.mean(y, axis=-1, keepdims=True)
    yc = y - mu
    yn = yc * lax.rsqrt(jnp.mean(yc * yc, axis=-1, keepdims=True) + GN_EPS) * gnw_ref[...]
    g = g_ref[...]
    o_ref[...] = (g * jax.nn.sigmoid(g)) * yn


def retention_mix(z, s0, gn_w, B, T, pos0):
    C = RET_CHUNK if T % RET_CHUNK == 0 else T
    NC = T // C
    half = RET_DK // 2
    pos = (pos0 + jnp.arange(T)).astype(F32)
    inv = ROPE_BASE ** (-jnp.arange(half, dtype=F32) / half)
    ang = pos[:, None] * inv[None, :]
    log_g = jnp.log1p(-jnp.exp2(-5.0 - jnp.arange(RET_HEADS, dtype=F32)))
    lg = jnp.broadcast_to(log_g[:, None, None], (RET_HEADS, 1, LANE))
    kb = RET_QK // RET_DK
    vb = 2 * RET_QK // RET_DV
    gb = (2 * RET_QK + RET_V) // RET_DV
    row = lambda b, h, c: b * NC + c
    return pl.pallas_call(
        functools.partial(_retention_kernel, C=C),
        out_shape=(jax.ShapeDtypeStruct((B * T, RET_V), F32),
                   jax.ShapeDtypeStruct((B, RET_HEADS, RET_DK, RET_DV), F32)),
        grid=(B, RET_HEADS, NC),
        in_specs=[pl.BlockSpec((C, RET_DK), lambda b, h, c: (row(b, h, c), h)),
                  pl.BlockSpec((C, RET_DK), lambda b, h, c: (row(b, h, c), kb + h)),
                  pl.BlockSpec((C, RET_DV), lambda b, h, c: (row(b, h, c), vb + h)),
                  pl.BlockSpec((C, RET_DV), lambda b, h, c: (row(b, h, c), gb + h)),
                  pl.BlockSpec((C, half), lambda b, h, c: (c, 0)),
                  pl.BlockSpec((C, half), lambda b, h, c: (c, 0)),
                  pl.BlockSpec((1, 1, LANE), lambda b, h, c: (h, 0, 0)),
                  pl.BlockSpec((1, RET_DV), lambda b, h, c: (0, h)),
                  pl.BlockSpec((1, 1, RET_DK, RET_DV), lambda b, h, c: (b, h, 0, 0))],
        out_specs=(pl.BlockSpec((C, RET_DV), lambda b, h, c: (row(b, h, c), h)),
                   pl.BlockSpec((1, 1, RET_DK, RET_DV), lambda b, h, c: (b, h, 0, 0))),
        compiler_params=_params("parallel", "parallel", "arbitrary"),
        name="retention_mix",
    )(z, z, z, z, jnp.cos(ang), jnp.sin(ang), lg, gn_w.reshape(1, RET_V), s0)


def _run_trunk(x, p, B, T, pos0, moba_fn, rwkv_s0, rwkv_shift0, ret_s0, G, C, W):
    h = x
    z = norm_matmul(h, W["norm_mix"][0], W["a_w_in"][0], tn=256)
    s0_cat = rwkv_s0.transpose(0, 2, 1, 3).reshape(B, HEAD_DIM, RWKV_WIDTH)
    rwkv_out, s_cat = rwkv_mix(z, rwkv_shift0, s0_cat, W["rwkv"], B, T, G, C)
    rwkv_state = s_cat.reshape(B, HEAD_DIM, RWKV_HEADS, HEAD_DIM).transpose(0, 2, 1, 3)
    zk = z[:, RWKV_PROJ + MOBA_WIDTH:RWKV_PROJ + 2 * MOBA_WIDTH]
    zv = z[:, RWKV_PROJ + 2 * MOBA_WIDTH:]
    moba_out = moba_fn(z, zk, zv)
    shift_t = z.reshape(B, T, A_IN)[:, T - 1, :RWKV_PROJ]
    wo = W["a_w_out"][0]
    h = proj_residual(h, [rwkv_out, moba_out], [wo[:RWKV_WIDTH], wo[RWKV_WIDTH:]])
    h = ffn_residual(h, W["norm_ffn"][0], W["ffn_w_gate"][0], W["ffn_w_up"][0], W["ffn_w_down"][0])
    h = ple_residual(h, W["ple_norm"][0], p[0], W["ple_proj"][0], W["ple_gate"][0],
                     W["final_norm"], final=False)
    z2 = norm_matmul(h, W["norm_mix"][1], W["c_w_in"][0], tn=512)
    gated, ret_state = retention_mix(z2, ret_s0, W["ret_gn_w"][0], B, T, pos0)
    h = proj_residual(h, [gated], [W["c_w_out"][0]])
    h = ffn_residual(h, W["norm_ffn"][1], W["ffn_w_gate"][1], W["ffn_w_up"][1], W["ffn_w_down"][1])
    y = ple_residual(h, W["ple_norm"][1], p[1], W["ple_proj"][1], W["ple_gate"][1],
                     W["final_norm"], final=True)
    k_rows = zk.reshape(1, B, T, MOBA_HEADS, HEAD_DIM)
    v_rows = zv.reshape(1, B, T, MOBA_HEADS, HEAD_DIM)
    return (y.reshape(B, T, D_MODEL), k_rows, v_rows, rwkv_state[None],
            shift_t[None], ret_state[None])


def kernel(x_prompt, x_sample, cache_moba_k, cache_moba_v, state_rwkv, state_rwkv_shift, state_ret, page_table, p_prompt, p_sample, norm_mix, norm_ffn, ffn_w_gate, ffn_w_up, ffn_w_down, ple_norm, ple_gate, ple_proj, a_w_in, rwkv_mu, rwkv_w0, rwkv_w2, rwkv_a0, rwkv_a2, rwkv_g2, rwkv_k_k, rwkv_k_a, rwkv_r_k, rwkv_ln_w, rwkv_ln_b, a_w_out, c_w_in, ret_gn_w, c_w_out, final_norm):
    assert norm_mix.shape[0] == DEPTH == 2
    B, T, _ = x_prompt.shape
    Bd, Td, _ = x_sample.shape
    bf = lambda w: w.astype(BF16)
    row = lambda a: a.reshape(1, RWKV_WIDTH)
    rwkv = dict(mu=rwkv_mu[0], w0=row(rwkv_w0[0]), a0=row(rwkv_a0[0]), g2=rwkv_g2[0],
                k_k=row(rwkv_k_k[0]), k_a=row(rwkv_k_a[0]), r_k=row(rwkv_r_k[0]),
                ln_w=row(rwkv_ln_w[0]), ln_b=row(rwkv_ln_b[0]),
                wwa=_pack_wwa(rwkv_w2[0], rwkv_a2[0]))
    W = dict(norm_mix=norm_mix, norm_ffn=norm_ffn, ffn_w_gate=bf(ffn_w_gate), ffn_w_up=bf(ffn_w_up),
             ffn_w_down=bf(ffn_w_down), ple_norm=ple_norm, ple_gate=bf(ple_gate), ple_proj=bf(ple_proj),
             a_w_in=bf(a_w_in), a_w_out=bf(a_w_out), c_w_in=bf(c_w_in), c_w_out=bf(c_w_out),
             ret_gn_w=ret_gn_w, final_norm=final_norm, rwkv=rwkv)

    yp, kp, vp, rsp, shp, rtp = _run_trunk(
        x_prompt.reshape(B * T, D_MODEL), p_prompt.reshape(DEPTH, B * T, PLE_DIM), B, T, 0,
        lambda z, zk, zv: moba_prompt(z, B, T),
        jnp.zeros((B, RWKV_HEADS, HEAD_DIM, HEAD_DIM), F32), jnp.zeros((B, RWKV_PROJ), F32),
        jnp.zeros((B, RET_HEADS, RET_DK, RET_DV), F32), 2, min(RWKV_CHUNK, T), W)

    past_len = page_table.shape[1] * cache_moba_k.shape[2]
    q_lo = RWKV_PROJ
    ys, ks, vs, rss, shs, rts = _run_trunk(
        x_sample.reshape(Bd * Td, D_MODEL), p_sample.reshape(DEPTH, Bd * Td, PLE_DIM), Bd, Td, past_len,
        lambda z, zk, zv: moba_sample(z[:, q_lo:q_lo + MOBA_WIDTH], zk, zv, cache_moba_k[0],
                                      cache_moba_v[0], page_table, Bd, Td),
        state_rwkv[0], state_rwkv_shift[0], state_ret[0], RWKV_HEADS, Td, W)

    return (yp, ys, kp, vp, ks, vs, rsp, rss, shp, shs, rtp, rts)
```

```python
import functools
import math

import numpy as np
import jax
import jax.numpy as jnp
from jax import lax
from jax.experimental import pallas as pl
from jax.experimental.pallas import tpu as pltpu

F32 = jnp.float32
BF16 = jnp.bfloat16
HI = lax.Precision.HIGHEST

D_MODEL = 1024
DEPTH = 2
PLE_DIM = 256
HEAD_DIM = 64
RWKV_HEADS = 8
RWKV_WIDTH = RWKV_HEADS * HEAD_DIM
W_LORA = 64
A_LORA = 64
G_LORA = 128
RWKV_PROJ = 3 * RWKV_WIDTH + W_LORA + A_LORA + G_LORA
RWKV_LN_EPS = 64e-5
RWKV_CHUNK = 64
MOBA_HEADS = 8
MOBA_WIDTH = MOBA_HEADS * HEAD_DIM
MOBA_BLOCK = 256
MOBA_TOPK = 3
MOBA_QCHUNK = 128
MOBA_KEY_GROUP = 4 * MOBA_BLOCK
A_IN = RWKV_PROJ + 3 * MOBA_WIDTH
RET_HEADS = 4
RET_DK = 256
RET_DV = 512
RET_QK = RET_HEADS * RET_DK
RET_V = RET_HEADS * RET_DV
C_IN = 2 * RET_QK + 2 * RET_V
RET_CHUNK = 128
ROPE_BASE = 10000.0
D_FF = 2816
NORM_EPS = 1e-6
GN_EPS = 1e-5

LANE = 128
VMEM_LIMIT = 56 * 1024 * 1024

NN = (((1,), (0,)), ((), ()))
NT = (((1,), (1,)), ((), ()))
TN = (((0,), (0,)), ((), ()))


def _params(*sem):
    return pltpu.CompilerParams(dimension_semantics=sem, vmem_limit_bytes=VMEM_LIMIT)


def _rms(x, g):
    return x * lax.rsqrt(jnp.mean(x * x, axis=-1, keepdims=True) + NORM_EPS) * g


def _row_tile(n):
    for t in (1024, 512, 256, 128, 64, 32, 16, 8):
        if n % t == 0:
            return t
    raise ValueError(n)


def _norm_matmul_kernel(x_ref, g_ref, w_ref, o_ref, xn_ref):
    @pl.when(pl.program_id(1) == 0)
    def _():
        xn_ref[...] = _rms(x_ref[...], g_ref[...]).astype(BF16)

    o_ref[...] = jnp.dot(xn_ref[...], w_ref[...], preferred_element_type=F32)


def norm_matmul(x, g, w, tn):
    n, d = x.shape
    f = w.shape[1]
    tm = _row_tile(n)
    return pl.pallas_call(
        _norm_matmul_kernel,
        out_shape=jax.ShapeDtypeStruct((n, f), F32),
        grid=(n // tm, f // tn),
        in_specs=[pl.BlockSpec((tm, d), lambda i, j: (i, 0)),
                  pl.BlockSpec((1, d), lambda i, j: (0, 0)),
                  pl.BlockSpec((d, tn), lambda i, j: (0, j))],
        out_specs=pl.BlockSpec((tm, tn), lambda i, j: (i, j)),
        scratch_shapes=[pltpu.VMEM((tm, d), BF16)],
        compiler_params=_params("parallel", "arbitrary"),
        name="norm_matmul",
    )(x, g.reshape(1, d), w)


def _proj_res_kernel(*refs, n_x):
    h_ref = refs[0]
    xs = refs[1:1 + n_x]
    ws = refs[1 + n_x:1 + 2 * n_x]
    o_ref = refs[1 + 2 * n_x]
    acc = h_ref[...]
    for x_ref, w_ref in zip(xs, ws):
        acc = acc + jnp.dot(x_ref[...].astype(BF16), w_ref[...], preferred_element_type=F32)
    o_ref[...] = acc


def proj_residual(h, xs, ws):
    n, d = h.shape
    tm = min(_row_tile(n), 512)
    in_specs = [pl.BlockSpec((tm, d), lambda i: (i, 0))]
    in_specs += [pl.BlockSpec((tm, x.shape[1]), lambda i: (i, 0)) for x in xs]
    in_specs += [pl.BlockSpec(w.shape, lambda i: (0, 0)) for w in ws]
    return pl.pallas_call(
        functools.partial(_proj_res_kernel, n_x=len(xs)),
        out_shape=jax.ShapeDtypeStruct((n, d), F32),
        grid=(n // tm,),
        in_specs=in_specs,
        out_specs=pl.BlockSpec((tm, d), lambda i: (i, 0)),
        compiler_params=_params("parallel"),
        name="proj_residual",
    )(h, *xs, *ws)


def _ffn_kernel(h_ref, g_ref, wg_ref, wu_ref, wd_ref, o_ref, xn_ref):
    @pl.when(pl.program_id(1) == 0)
    def _():
        x = h_ref[...]
        xn_ref[...] = _rms(x, g_ref[...]).astype(BF16)
        o_ref[...] = x

    xn = xn_ref[...]
    a = jnp.dot(xn, wg_ref[...], preferred_element_type=F32)
    b = jnp.dot(xn, wu_ref[...], preferred_element_type=F32)
    t = (a * jax.nn.sigmoid(a)) * b
    o_ref[...] += jnp.dot(t.astype(BF16), wd_ref[...], preferred_element_type=F32)


def ffn_residual(h, g, wg, wu, wd, tf=256):
    n, d = h.shape
    ff = wg.shape[1]
    tm = _row_tile(n)
    return pl.pallas_call(
        _ffn_kernel,
        out_shape=jax.ShapeDtypeStruct((n, d), F32),
        grid=(n // tm, ff // tf),
        in_specs=[pl.BlockSpec((tm, d), lambda i, j: (i, 0)),
                  pl.BlockSpec((1, d), lambda i, j: (0, 0)),
                  pl.BlockSpec((d, tf), lambda i, j: (0, j)),
                  pl.BlockSpec((d, tf), lambda i, j: (0, j)),
                  pl.BlockSpec((tf, d), lambda i, j: (j, 0))],
        out_specs=pl.BlockSpec((tm, d), lambda i, j: (i, 0)),
        scratch_shapes=[pltpu.VMEM((tm, d), BF16)],
        compiler_params=_params("parallel", "arbitrary"),
        name="ffn_residual",
    )(h, g.reshape(1, d), wg, wu, wd)


def _ple_kernel(h_ref, g_ref, p_ref, wp_ref, wgate_ref, fn_ref, o_ref, *, final):
    x = h_ref[...]
    xn = _rms(x, g_ref[...]).astype(BF16)
    gate = jax.nn.sigmoid(jnp.dot(xn, wgate_ref[...], preferred_element_type=F32))
    pp = jnp.dot(p_ref[...].astype(BF16), wp_ref[...], preferred_element_type=F32)
    hn = x + pp * gate
    if final:
        hn = _rms(hn, fn_ref[...])
    o_ref[...] = hn


def ple_residual(h, g, p, wp, wgate, final_g, final):
    n, d = h.shape
    pd = p.shape[1]
    tm = min(_row_tile(n), 512)
    return pl.pallas_call(
        functools.partial(_ple_kernel, final=final),
        out_shape=jax.ShapeDtypeStruct((n, d), F32),
        grid=(n // tm,),
        in_specs=[pl.BlockSpec((tm, d), lambda i: (i, 0)),
                  pl.BlockSpec((1, d), lambda i: (0, 0)),
                  pl.BlockSpec((tm, pd), lambda i: (i, 0)),
                  pl.BlockSpec((pd, d), lambda i: (0, 0)),
                  pl.BlockSpec((d, d), lambda i: (0, 0)),
                  pl.BlockSpec((1, d), lambda i: (0, 0))],
        out_specs=pl.BlockSpec((tm, d), lambda i: (i, 0)),
        compiler_params=_params("parallel"),
        name="ple_residual",
    )(h, g.reshape(1, d), p, wp, wgate, final_g.reshape(1, d))


def _softplus(x):
    return jnp.maximum(x, 0.0) + jnp.log1p(jnp.exp(-jnp.abs(x)))


def _split2(x):
    hi = x.astype(BF16)
    return hi, (x - hi.astype(F32)).astype(BF16)


def _split3(x):
    hi = x.astype(BF16)
    r1 = x - hi.astype(F32)
    mid = r1.astype(BF16)
    return hi, mid, (r1 - mid.astype(F32)).astype(BF16)


def _dot3(a, b, dims=NN):
    ah, al = _split2(a)
    bh, bl = _split2(b)
    d = lambda x, y: lax.dot_general(x, y, dims, preferred_element_type=F32)
    return d(ah, bh) + (d(ah, bl) + d(al, bh))


def _dot_mask(a, m, mask_left=False):
    d = lambda x: lax.dot_general(m, x, NN, preferred_element_type=F32) if mask_left else \
        lax.dot_general(x, m, NN, preferred_element_type=F32)
    hi, mid, lo = _split3(a)
    return d(hi) + (d(mid) + d(lo))


def _rwkv_kernel(zr_ref, zk_ref, zv_ref, zl_ref, sr_ref, sk_ref, sv_ref, sl_ref,
                 mur_ref, muk_ref, muv_ref, mul_ref, w0_ref, wwa_ref, a0_ref, g2_ref,
                 kkw_ref, kaw_ref, rkw_ref, lnw_ref, lnb_ref, s0_ref,
                 hm_ref, mst_ref, min_ref, tri_ref, bd_ref,
                 out_ref, s_out_ref, lr_ref, lk_ref, lv_ref, ll_ref, sbd_ref, *, G, C):
    L = G * HEAD_DIM
    R = G * C
    NG = RWKV_HEADS // G
    c = pl.program_id(1)
    wide = R % LANE == 0

    @pl.when(c == 0)
    def _():
        lr_ref[...] = sr_ref[0]
        lk_ref[...] = sk_ref[0]
        lv_ref[...] = sv_ref[0]
        ll_ref[...] = sl_ref[0]
        bdm = bd_ref[:L, :L].astype(F32)
        for gi in range(NG):
            sc = s0_ref[0, :, gi * L:(gi + 1) * L]
            sbd_ref[gi] = jnp.concatenate([sc] * G, axis=0) * bdm

    def shift_mix(x_ref, last_ref, mu_ref):
        x = x_ref[...]
        row = lax.broadcasted_iota(jnp.int32, x.shape, 0)
        prev = jnp.where(row == 0, last_ref[...], pltpu.roll(x, 1, axis=0))
        last_ref[...] = x[C - 1:C, :]
        return x + mu_ref[...] * (prev - x)

    r = shift_mix(zr_ref, lr_ref, mur_ref)
    k = shift_mix(zk_ref, lk_ref, muk_ref)
    v = shift_mix(zv_ref, lv_ref, muv_ref)
    lo = shift_mix(zl_ref, ll_ref, mul_ref)

    wa_in = lo[:, :W_LORA + A_LORA]
    lane = lax.broadcasted_iota(jnp.int32, wa_in.shape, 1)
    wa_in = jnp.where(lane < W_LORA, jnp.tanh(wa_in), wa_in)
    wa = jnp.dot(wa_in, wwa_ref[...], preferred_element_type=F32)
    logw = -_softplus(-(w0_ref[...] + wa[:, :RWKV_WIDTH])) - 0.5
    ld = -jnp.exp(logw)
    a = jax.nn.sigmoid(a0_ref[...] + wa[:, RWKV_WIDTH:])
    g = jnp.dot(jax.nn.sigmoid(lo[:, W_LORA + A_LORA:]), g2_ref[...], preferred_element_type=F32)

    kkv = k * kkw_ref[...]
    k2 = k * (1.0 + (a - 1.0) * kaw_ref[...])
    sums = _dot_mask(jnp.concatenate([kkv * kkv, r * k2 * rkw_ref[...]], axis=0), bd_ref[...])
    kkn = kkv / jnp.maximum(jnp.sqrt(sums[:C]), 1e-12)
    bonus = sums[C:] * v
    an = -kkn
    bn = kkn * a

    cs = _dot_mask(ld, tri_ref[...], mask_left=True)
    e_neg = jnp.exp(-cs)
    e_pos = jnp.exp(cs)
    at = an * jnp.exp(cs - ld)
    rt = r * e_pos
    bt = bn * e_neg
    kt = k2 * e_neg
    pc = e_pos[C - 1:C, :]

    hm = hm_ref[...]
    strict = mst_ref[...] > 0.5
    incl = min_ref[...] > 0.5
    n_dbl = max(1, math.ceil(math.log2(C)))
    groups = range(NG)
    sls = [slice(gi * L, (gi + 1) * L) for gi in groups]
    stack = lambda x, sl: jnp.concatenate([x[:, sl]] * G, axis=0) * hm
    a_s = [stack(at, sl) for sl in sls]
    r_s = [stack(rt, sl) for sl in sls]
    b_s = [stack(bt, sl) for sl in sls]
    k_s = [stack(kt, sl) for sl in sls]
    v_s = [stack(v, sl) for sl in sls]
    lhs = [jnp.concatenate([a_s[i], r_s[i]], axis=0) for i in groups]
    bk = [jnp.concatenate([b_s[i], k_s[i]], axis=0) for i in groups]
    if wide:
        m = [_dot3(lhs[i], bk[i], NT) for i in groups]
        m_b = [x[:, :R] for x in m]
        m_k = [x[:, R:] for x in m]
    else:
        m_b = [_dot3(lhs[i], b_s[i], NT) for i in groups]
        m_k = [_dot3(lhs[i], k_s[i], NT) for i in groups]
    a_ab = [jnp.where(strict, x[:R], 0.0) for x in m_b]
    a_ak = [jnp.where(strict, x[:R], 0.0) for x in m_k]
    a_rb = [jnp.where(incl, x[R:], 0.0) for x in m_b]
    a_rk = [jnp.where(incl, x[R:], 0.0) for x in m_k]

    sb = [sbd_ref[i] for i in groups]
    w0s = [_dot3(lhs[i], sb[i], NT) for i in groups]
    u = [w0s[i][:R] + _dot3(a_ak[i], v_s[i]) for i in groups]
    ap = a_ab
    for j in range(n_dbl):
        if j + 1 == n_dbl:
            u = [u[i] + _dot3(ap[i], u[i]) for i in groups]
        elif wide:
            t = [_dot3(ap[i], jnp.concatenate([u[i], ap[i]], axis=1)) for i in groups]
            u = [u[i] + t[i][:, :L] for i in groups]
            ap = [t[i][:, L:] for i in groups]
        else:
            u, ap = ([u[i] + _dot3(ap[i], u[i]) for i in groups], [_dot3(ap[i], ap[i]) for i in groups])
    if wide:
        y = [w0s[i][R:] + _dot3(jnp.concatenate([a_rb[i], a_rk[i]], axis=1),
                                jnp.concatenate([u[i], v_s[i]], axis=0)) for i in groups]
    else:
        y = [w0s[i][R:] + _dot3(a_rb[i], u[i]) + _dot3(a_rk[i], v_s[i]) for i in groups]

    ys = []
    for i in groups:
        mu_y = jnp.sum(y[i], axis=-1, keepdims=True) * (1.0 / HEAD_DIM)
        yc = (y[i] - mu_y) * hm
        yn = yc * lax.rsqrt(jnp.sum(yc * yc, axis=-1, keepdims=True) * (1.0 / HEAD_DIM) + RWKV_LN_EPS)
        y_cl = yn[:C]
        for h in range(1, G):
            y_cl = y_cl + yn[h * C:(h + 1) * C]
        ys.append(y_cl)

    uv = [jnp.concatenate([u[i], v_s[i]], axis=0) for i in groups]
    s_new = [(sb[i] + _dot3(uv[i], bk[i], TN)) * pc[:, sls[i]] for i in groups]
    for i in groups:
        sbd_ref[i] = s_new[i]
        s_cat = s_new[i][:HEAD_DIM]
        for h in range(1, G):
            s_cat = s_cat + s_new[i][h * HEAD_DIM:(h + 1) * HEAD_DIM]
        s_out_ref[0, :, sls[i]] = s_cat
    y_all = ys[0] if NG == 1 else jnp.concatenate(ys, axis=1)
    out_ref[...] = (y_all * lnw_ref[...] + lnb_ref[...] + bonus) * g


def _rwkv_tables(G, C):
    L, R = G * HEAD_DIM, G * C
    i = np.arange(R)[:, None]
    j = np.arange(R)[None, :]
    l = np.arange(RWKV_WIDTH)
    same = (i // C) == (j // C)
    hm = ((np.arange(R)[:, None] // C) == (np.arange(L)[None, :] // HEAD_DIM))
    tri = np.arange(C)[:, None] >= np.arange(C)[None, :]
    bd = (l[:, None] // HEAD_DIM) == (l[None, :] // HEAD_DIM)
    f = lambda m, dt=np.float32: jnp.asarray(m.astype(np.float32), dtype=dt)
    return [f(hm), f(same & (i > j)), f(same & (i >= j)), f(tri, BF16), f(bd, BF16)]


def rwkv_mix(z, shift0, s0_cat, prm, B, T, G, C):
    L = G * HEAD_DIM
    NG = RWKV_HEADS // G
    NC = T // C
    W = RWKV_WIDTH
    lw = W_LORA + A_LORA + G_LORA
    lb = (3 * W) // lw
    row = lambda b, c: b * NC + c
    zspec = lambda off: pl.BlockSpec((C, W), lambda b, c: (row(b, c), off))
    sspec = lambda off: pl.BlockSpec((1, 1, W), lambda b, c: (b, 0, off))
    vspec = lambda off: pl.BlockSpec((1, W), lambda b, c: (0, off))
    const = lambda a: pl.BlockSpec(a.shape, lambda b, c: (0,) * a.ndim)
    tables = _rwkv_tables(G, C)
    mu = prm["mu"].reshape(1, RWKV_PROJ)
    sh = shift0.reshape(B, 1, RWKV_PROJ)
    in_specs = [zspec(0), zspec(1), zspec(2),
                pl.BlockSpec((C, lw), lambda b, c: (row(b, c), lb)),
                sspec(0), sspec(1), sspec(2),
                pl.BlockSpec((1, 1, lw), lambda b, c: (b, 0, lb)),
                vspec(0), vspec(1), vspec(2),
                pl.BlockSpec((1, lw), lambda b, c: (0, lb)),
                vspec(0),
                const(prm["wwa"]),
                vspec(0),
                const(prm["g2"]),
                vspec(0), vspec(0), vspec(0), vspec(0), vspec(0),
                pl.BlockSpec((1, HEAD_DIM, W), lambda b, c: (b, 0, 0))]
    in_specs += [const(t) for t in tables]
    out, s_t = pl.pallas_call(
        functools.partial(_rwkv_kernel, G=G, C=C),
        out_shape=(jax.ShapeDtypeStruct((B * T, W), F32),
                   jax.ShapeDtypeStruct((B, HEAD_DIM, W), F32)),
        grid=(B, NC),
        in_specs=in_specs,
        out_specs=(pl.BlockSpec((C, W), lambda b, c: (row(b, c), 0)),
                   pl.BlockSpec((1, HEAD_DIM, W), lambda b, c: (b, 0, 0))),
        scratch_shapes=[pltpu.VMEM((1, W), F32), pltpu.VMEM((1, W), F32),
                        pltpu.VMEM((1, W), F32), pltpu.VMEM((1, lw), F32),
                        pltpu.VMEM((NG, L, L), F32)],
        compiler_params=_params("parallel", "arbitrary"),
        name="rwkv_mix",
    )(z, z, z, z, sh, sh, sh, sh, mu, mu, mu, mu,
      prm["w0"], prm["wwa"], prm["a0"], prm["g2"], prm["k_k"], prm["k_a"], prm["r_k"],
      prm["ln_w"], prm["ln_b"], s0_cat, *tables)
    return out, s_t


def _pack_wwa(w2, a2):
    top = jnp.concatenate([w2, jnp.zeros_like(w2)], axis=1)
    bot = jnp.concatenate([jnp.zeros_like(a2), a2], axis=1)
    return jnp.concatenate([top, bot], axis=0)


def _top_blocks(gate, n_iota, n_valid, n_blocks, axis=-1):
    neg = -jnp.inf
    valid = n_iota < n_valid
    g = jnp.where(valid, gate, neg)
    sel = jnp.zeros(gate.shape, F32)
    for _ in range(MOBA_TOPK):
        m = jnp.max(g, axis=axis, keepdims=True)
        idx = jnp.min(jnp.where(g == m, n_iota, n_blocks), axis=axis, keepdims=True)
        pick = n_iota == idx
        sel = jnp.where(jnp.logical_and(pick, valid), 1.0, sel)
        g = jnp.where(pick, neg, g)
    return sel


def _moba_prompt_kernel(q_ref, k_ref, v_ref, o_ref, kbf_ref, vt_ref, kmean_ref, *, T):
    NB = T // MOBA_BLOCK
    QB = MOBA_BLOCK
    KG = MOBA_KEY_GROUP
    BPG = KG // MOBA_BLOCK
    NH = LANE // HEAD_DIM
    blk = pl.program_id(2)
    scale = HEAD_DIM ** -0.5

    @pl.when(blk == 0)
    def _():
        kf = k_ref[...]
        kbf_ref[...] = kf.astype(BF16)
        kmean_ref[...] = jnp.sum(kf.reshape(NB, MOBA_BLOCK, LANE), axis=1) * (1.0 / MOBA_BLOCK)
        for g in range(T // KG):
            vt_ref[g] = v_ref[g * KG:(g + 1) * KG, :].T.astype(BF16)

    q = q_ref[...]
    lane = lax.broadcasted_iota(jnp.int32, (QB, LANE), 1)
    n_iota = lax.broadcasted_iota(jnp.int32, (NB, QB), 0)
    kmean = kmean_ref[...]
    qbs, sels = [], []
    for hh in range(NH):
        qm = jnp.where((lane // HEAD_DIM) == hh, q, 0.0)
        gate = lax.dot_general(kmean, qm, NT, precision=HI, preferred_element_type=F32)
        qbs.append((qm * scale).astype(BF16))
        sels.append(_top_blocks(gate, n_iota, blk, NB, axis=0))

    kidx = lax.broadcasted_iota(jnp.int32, (MOBA_BLOCK, QB), 0)
    qidx = lax.broadcasted_iota(jnp.int32, (MOBA_BLOCK, QB), 1)
    n_groups = (blk * QB) // KG + 1

    def step(gi, carry, own_group):
        st = pl.multiple_of(gi * KG, KG)
        kg = kbf_ref[pl.ds(st, KG), :]
        vtg = vt_ref[gi]
        heads = range(NH)
        s = [lax.dot_general(kg, qbs[hh], NT, preferred_element_type=F32) for hh in heads]

        def mask(hh):
            parts = []
            for j in range(BPG):
                n = gi * BPG + j
                ok = jnp.max(jnp.where(n_iota == n, sels[hh], 0.0), axis=0, keepdims=True) > 0.5
                if own_group:
                    ok = jnp.logical_or(ok, jnp.logical_and(n == blk, kidx <= qidx))
                parts.append(jnp.where(ok, s[hh][j * MOBA_BLOCK:(j + 1) * MOBA_BLOCK], -jnp.inf))
            return jnp.concatenate(parts, axis=0)

        s = [mask(hh) for hh in heads]
        m_new = [jnp.maximum(carry[hh][0], jnp.max(s[hh], axis=0, keepdims=True)) for hh in heads]
        alpha = [jnp.exp(carry[hh][0] - m_new[hh]) for hh in heads]
        p = [jnp.exp(s[hh] - m_new[hh]) for hh in heads]
        l = [alpha[hh] * carry[hh][1] + jnp.sum(p[hh], axis=0, keepdims=True) for hh in heads]
        pv = [jnp.dot(vtg, p[hh].astype(BF16), preferred_element_type=F32) for hh in heads]
        return tuple((m_new[hh], l[hh], alpha[hh] * carry[hh][2] + pv[hh]) for hh in heads)

    init = tuple((jnp.full((1, QB), -jnp.inf, F32), jnp.zeros((1, QB), F32), jnp.zeros((LANE, QB), F32))
                 for _ in range(NH))
    first = step(n_groups - 1, init, True)
    res = lax.fori_loop(1, n_groups, lambda i, carry: step(n_groups - 1 - i, carry, False), first)
    rowi = lax.broadcasted_iota(jnp.int32, (LANE, QB), 0)
    out_t = jnp.zeros((LANE, QB), F32)
    for hh in range(NH):
        _, l, acc = res[hh]
        out_t = jnp.where((rowi // HEAD_DIM) == hh, acc / l, out_t)
    o_ref[...] = out_t.T


def moba_prompt(z, B, T):
    assert T % MOBA_KEY_GROUP == 0 and MOBA_BLOCK % MOBA_QCHUNK == 0
    NB = T // MOBA_BLOCK
    NKG = T // MOBA_KEY_GROUP
    P = MOBA_WIDTH // LANE
    q0 = RWKV_PROJ // LANE
    k0 = (RWKV_PROJ + MOBA_WIDTH) // LANE
    v0 = (RWKV_PROJ + 2 * MOBA_WIDTH) // LANE
    return pl.pallas_call(
        functools.partial(_moba_prompt_kernel, T=T),
        out_shape=jax.ShapeDtypeStruct((B * T, MOBA_WIDTH), F32),
        grid=(B, P, NB),
        in_specs=[pl.BlockSpec((MOBA_BLOCK, LANE), lambda b, p, c: (b * NB + c, q0 + p)),
                  pl.BlockSpec((T, LANE), lambda b, p, c: (b, k0 + p)),
                  pl.BlockSpec((T, LANE), lambda b, p, c: (b, v0 + p))],
        out_specs=pl.BlockSpec((MOBA_BLOCK, LANE), lambda b, p, c: (b * NB + c, p)),
        scratch_shapes=[pltpu.VMEM((T, LANE), BF16), pltpu.VMEM((NKG, LANE, MOBA_KEY_GROUP), BF16),
                        pltpu.VMEM((NB, LANE), F32)],
        compiler_params=_params("parallel", "parallel", "arbitrary"),
        name="moba_prompt",
    )(z, z, z)


def _moba_sample_kernel(pt_ref, q_ref, kn_ref, vn_ref, *refs, n_pages, tq):
    kp = refs[:n_pages]
    vp = refs[n_pages:2 * n_pages]
    o_ref = refs[2 * n_pages]
    H = MOBA_HEADS
    R = H * tq
    page = kp[0].shape[1]
    PR = page * H
    ppb = MOBA_BLOCK // page
    n_past = n_pages // ppb
    scale = HEAD_DIM ** -0.5
    q = q_ref[...]

    means = []
    for n in range(n_past):
        tot = jnp.sum(kp[n * ppb][0], axis=0)
        for j in range(1, ppb):
            tot = tot + jnp.sum(kp[n * ppb + j][0], axis=0)
        means.append(tot * (1.0 / MOBA_BLOCK))
    kmean = jnp.concatenate(means, axis=0)
    g_all = lax.dot_general(q, kmean, NT, precision=HI, preferred_element_type=F32)
    gr = lax.broadcasted_iota(jnp.int32, g_all.shape, 0) // tq
    gc = lax.broadcasted_iota(jnp.int32, g_all.shape, 1)
    g_own = jnp.where(gr == gc % H, g_all, 0.0)
    fold = (lax.broadcasted_iota(jnp.int32, (n_past * H, n_past), 0) // H
            == lax.broadcasted_iota(jnp.int32, (n_past * H, n_past), 1))
    gate = _dot_mask(g_own, jnp.where(fold, 1.0, 0.0).astype(BF16))
    n_iota = lax.broadcasted_iota(jnp.int32, (R, n_past), 1)
    sel = _top_blocks(gate, n_iota, n_past, n_past + 1)

    qb = (q * scale).astype(BF16)
    same_head = (lax.broadcasted_iota(jnp.int32, (R, PR), 0) // tq
                 == lax.broadcasted_iota(jnp.int32, (R, PR), 1) % H)
    scores = []
    for j in range(n_pages):
        k2 = kp[j][0].reshape(PR, HEAD_DIM).astype(BF16)
        s = lax.dot_general(qb, k2, NT, preferred_element_type=F32)
        ok = jnp.logical_and(same_head, sel[:, j // ppb:j // ppb + 1] > 0.5)
        scores.append(jnp.where(ok, s, -jnp.inf))
    s_own = lax.dot_general(qb, kn_ref[...].astype(BF16), NT, preferred_element_type=F32)
    orow = lax.broadcasted_iota(jnp.int32, s_own.shape, 0)
    ocol = lax.broadcasted_iota(jnp.int32, s_own.shape, 1)
    ok_own = jnp.logical_and(orow // tq == ocol % H, ocol // H <= orow % tq)
    s_own = jnp.where(ok_own, s_own, -jnp.inf)

    m = jnp.max(s_own, axis=-1, keepdims=True)
    for s in scores:
        m = jnp.maximum(m, jnp.max(s, axis=-1, keepdims=True))
    p_own = jnp.exp(s_own - m)
    l = jnp.sum(p_own, axis=-1, keepdims=True)
    acc = jnp.dot(p_own.astype(BF16), vn_ref[...].astype(BF16), preferred_element_type=F32)
    for j in range(n_pages):
        p = jnp.exp(scores[j] - m)
        l = l + jnp.sum(p, axis=-1, keepdims=True)
        v2 = vp[j][0].reshape(PR, HEAD_DIM).astype(BF16)
        acc = acc + jnp.dot(p.astype(BF16), v2, preferred_element_type=F32)
    o_ref[...] = acc / l


def moba_sample(zq, zk, zv, pool_k, pool_v, page_table, B, T):
    n_pages = page_table.shape[1]
    n_pool, page_size = pool_k.shape[0], pool_k.shape[1]
    assert MOBA_BLOCK % page_size == 0 and (n_pages * page_size) % MOBA_BLOCK == 0
    assert T <= MOBA_BLOCK and T % MOBA_QCHUNK != 0 and n_pages * page_size // MOBA_BLOCK >= MOBA_TOPK
    H = MOBA_HEADS
    R = H * T
    heads_first = lambda x: x.reshape(B, T, H, HEAD_DIM).transpose(0, 2, 1, 3).reshape(B * R, HEAD_DIM)
    rows = pl.BlockSpec((R, HEAD_DIM), lambda b, pt: (b, 0))
    page = lambda j: pl.BlockSpec((1, page_size, H, HEAD_DIM), lambda b, pt: (pt[b, j], 0, 0, 0))
    out = pl.pallas_call(
        functools.partial(_moba_sample_kernel, n_pages=n_pages, tq=T),
        out_shape=jax.ShapeDtypeStruct((B * R, HEAD_DIM), F32),
        grid_spec=pltpu.PrefetchScalarGridSpec(
            num_scalar_prefetch=1,
            grid=(B,),
            in_specs=[rows, rows, rows] + [page(j) for j in range(n_pages)] * 2,
            out_specs=rows),
        compiler_params=_params("parallel"),
        name="moba_sample",
    )(page_table, heads_first(zq), zk.reshape(B * R, HEAD_DIM), zv.reshape(B * R, HEAD_DIM),
      *([pool_k] * n_pages), *([pool_v] * n_pages))
    return out.reshape(B, H, T, HEAD_DIM).transpose(0, 2, 1, 3).reshape(B * T, MOBA_WIDTH)


def _retention_kernel(q_ref, k_ref, v_ref, g_ref, cos_ref, sin_ref, lg_ref, gnw_ref, s0_ref,
                      o_ref, s_out_ref, *, C):
    c = pl.program_id(1)

    @pl.when(c == 0)
    def _():
        s_out_ref[...] = s0_ref[...]

    half = RET_DK // 2
    heads = range(RET_HEADS)
    cos = cos_ref[...]
    sin = sin_ref[...]

    def rot(x_ref, h):
        x1 = x_ref[:, h * RET_DK:h * RET_DK + half]
        x2 = x_ref[:, h * RET_DK + half:(h + 1) * RET_DK]
        return jnp.concatenate([x1 * cos - x2 * sin, x1 * sin + x2 * cos], axis=-1)

    ii = lax.broadcasted_iota(jnp.int32, (C, C), 0)
    jj = lax.broadcasted_iota(jnp.int32, (C, C), 1)
    diff = jnp.maximum((ii - jj).astype(F32), 0.0)
    it = lax.broadcasted_iota(jnp.int32, (C, LANE), 0).astype(F32)
    lg = [lg_ref[h] for h in heads]
    dmask = [jnp.where(ii >= jj, jnp.exp(diff * lg[h][:, :C]), 0.0) for h in heads]
    cross = [jnp.exp((it + 1.0) * lg[h])[:, :1] for h in heads]
    kdec = [jnp.exp((C - 1.0 - it) * lg[h])[:, :1] for h in heads]
    sdec = [jnp.exp(C * lg[h])[:, :1] for h in heads]

    qb = [rot(q_ref, h).astype(BF16) for h in heads]
    kr = [rot(k_ref, h) * (RET_DK ** -0.5) for h in heads]
    v = [v_ref[:, h * RET_DV:(h + 1) * RET_DV] for h in heads]
    vb = [x.astype(BF16) for x in v]
    s = [s_out_ref[0, h] for h in heads]
    att = [lax.dot_general(qb[h], kr[h].astype(BF16), NT, preferred_element_type=F32) * dmask[h]
           for h in heads]
    y_in = [jnp.dot(att[h].astype(BF16), vb[h], preferred_element_type=F32) for h in heads]
    y_x = [jnp.dot(qb[h], s[h].astype(BF16), preferred_element_type=F32) * cross[h] for h in heads]
    for h in heads:
        s_out_ref[0, h] = s[h] * sdec[h] + lax.dot_general(kr[h] * kdec[h], v[h], TN,
                                                           preferred_element_type=F32)
    for h in heads:
        y = y_in[h] + y_x[h]
        mu = jnp.mean(y, axis=-1, keepdims=True)
        yc = y - mu
        sl = slice(h * RET_DV, (h + 1) * RET_DV)
        yn = yc * lax.rsqrt(jnp.mean(yc * yc, axis=-1, keepdims=True) + GN_EPS) * gnw_ref[:, sl]
        g = g_ref[:, sl]
        o_ref[:, sl] = (g * jax.nn.sigmoid(g)) * yn


def retention_mix(z, s0, gn_w, B, T, pos0):
    C = RET_CHUNK if T % RET_CHUNK == 0 else T
    NC = T // C
    half = RET_DK // 2
    pos = (pos0 + jnp.arange(T)).astype(F32)
    inv = ROPE_BASE ** (-jnp.arange(half, dtype=F32) / half)
    ang = pos[:, None] * inv[None, :]
    log_g = jnp.log1p(-jnp.exp2(-5.0 - jnp.arange(RET_HEADS, dtype=F32)))
    lg = jnp.broadcast_to(log_g[:, None, None], (RET_HEADS, 1, LANE))
    row = lambda b, c: b * NC + c
    state = pl.BlockSpec((1, RET_HEADS, RET_DK, RET_DV), lambda b, c: (b, 0, 0, 0))
    return pl.pallas_call(
        functools.partial(_retention_kernel, C=C),
        out_shape=(jax.ShapeDtypeStruct((B * T, RET_V), F32),
                   jax.ShapeDtypeStruct((B, RET_HEADS, RET_DK, RET_DV), F32)),
        grid=(B, NC),
        in_specs=[pl.BlockSpec((C, RET_QK), lambda b, c: (row(b, c), 0)),
                  pl.BlockSpec((C, RET_QK), lambda b, c: (row(b, c), 1)),
                  pl.BlockSpec((C, RET_V), lambda b, c: (row(b, c), 2 * RET_QK // RET_V)),
                  pl.BlockSpec((C, RET_V), lambda b, c: (row(b, c), 2 * RET_QK // RET_V + 1)),
                  pl.BlockSpec((C, half), lambda b, c: (c, 0)),
                  pl.BlockSpec((C, half), lambda b, c: (c, 0)),
                  pl.BlockSpec((RET_HEADS, 1, LANE), lambda b, c: (0, 0, 0)),
                  pl.BlockSpec((1, RET_V), lambda b, c: (0, 0)),
                  state],
        out_specs=(pl.BlockSpec((C, RET_V), lambda b, c: (row(b, c), 0)), state),
        compiler_params=_params("parallel", "arbitrary"),
        name="retention_mix",
    )(z, z, z, z, jnp.cos(ang), jnp.sin(ang), lg, gn_w.reshape(1, RET_V), s0)


def _run_trunk(x, p, B, T, pos0, moba_fn, rwkv_s0, rwkv_shift0, ret_s0, G, C, W):
    h = x
    z = norm_matmul(h, W["norm_mix"][0], W["a_w_in"][0], tn=256)
    s0_cat = rwkv_s0.transpose(0, 2, 1, 3).reshape(B, HEAD_DIM, RWKV_WIDTH)
    rwkv_out, s_cat = rwkv_mix(z, rwkv_shift0, s0_cat, W["rwkv"], B, T, G, C)
    rwkv_state = s_cat.reshape(B, HEAD_DIM, RWKV_HEADS, HEAD_DIM).transpose(0, 2, 1, 3)
    zk = z[:, RWKV_PROJ + MOBA_WIDTH:RWKV_PROJ + 2 * MOBA_WIDTH]
    zv = z[:, RWKV_PROJ + 2 * MOBA_WIDTH:]
    moba_out = moba_fn(z, zk, zv)
    shift_t = z.reshape(B, T, A_IN)[:, T - 1, :RWKV_PROJ]
    wo = W["a_w_out"][0]
    h = proj_residual(h, [rwkv_out, moba_out], [wo[:RWKV_WIDTH], wo[RWKV_WIDTH:]])
    h = ffn_residual(h, W["norm_ffn"][0], W["ffn_w_gate"][0], W["ffn_w_up"][0], W["ffn_w_down"][0])
    h = ple_residual(h, W["ple_norm"][0], p[0], W["ple_proj"][0], W["ple_gate"][0],
                     W["final_norm"], final=False)
    z2 = norm_matmul(h, W["norm_mix"][1], W["c_w_in"][0], tn=512)
    gated, ret_state = retention_mix(z2, ret_s0, W["ret_gn_w"][0], B, T, pos0)
    h = proj_residual(h, [gated], [W["c_w_out"][0]])
    h = ffn_residual(h, W["norm_ffn"][1], W["ffn_w_gate"][1], W["ffn_w_up"][1], W["ffn_w_down"][1])
    y = ple_residual(h, W["ple_norm"][1], p[1], W["ple_proj"][1], W["ple_gate"][1],
                     W["final_norm"], final=True)
    k_rows = zk.reshape(1, B, T, MOBA_HEADS, HEAD_DIM)
    v_rows = zv.reshape(1, B, T, MOBA_HEADS, HEAD_DIM)
    return (y.reshape(B, T, D_MODEL), k_rows, v_rows, rwkv_state[None],
            shift_t[None], ret_state[None])


def kernel(x_prompt, x_sample, cache_moba_k, cache_moba_v, state_rwkv, state_rwkv_shift, state_ret, page_table, p_prompt, p_sample, norm_mix, norm_ffn, ffn_w_gate, ffn_w_up, ffn_w_down, ple_norm, ple_gate, ple_proj, a_w_in, rwkv_mu, rwkv_w0, rwkv_w2, rwkv_a0, rwkv_a2, rwkv_g2, rwkv_k_k, rwkv_k_a, rwkv_r_k, rwkv_ln_w, rwkv_ln_b, a_w_out, c_w_in, ret_gn_w, c_w_out, final_norm):
    assert norm_mix.shape[0] == DEPTH == 2
    B, T, _ = x_prompt.shape
    Bd, Td, _ = x_sample.shape
    bf = lambda w: w.astype(BF16)
    row = lambda a: a.reshape(1, RWKV_WIDTH)
    rwkv = dict(mu=rwkv_mu[0], w0=row(rwkv_w0[0]), a0=row(rwkv_a0[0]), g2=rwkv_g2[0],
                k_k=row(rwkv_k_k[0]), k_a=row(rwkv_k_a[0]), r_k=row(rwkv_r_k[0]),
                ln_w=row(rwkv_ln_w[0]), ln_b=row(rwkv_ln_b[0]),
                wwa=_pack_wwa(rwkv_w2[0], rwkv_a2[0]))
    W = dict(norm_mix=norm_mix, norm_ffn=norm_ffn, ffn_w_gate=bf(ffn_w_gate), ffn_w_up=bf(ffn_w_up),
             ffn_w_down=bf(ffn_w_down), ple_norm=ple_norm, ple_gate=bf(ple_gate), ple_proj=bf(ple_proj),
             a_w_in=bf(a_w_in), a_w_out=bf(a_w_out), c_w_in=bf(c_w_in), c_w_out=bf(c_w_out),
             ret_gn_w=ret_gn_w, final_norm=final_norm, rwkv=rwkv)

    yp, kp, vp, rsp, shp, rtp = _run_trunk(
        x_prompt.reshape(B * T, D_MODEL), p_prompt.reshape(DEPTH, B * T, PLE_DIM), B, T, 0,
        lambda z, zk, zv: moba_prompt(z, B, T),
        jnp.zeros((B, RWKV_HEADS, HEAD_DIM, HEAD_DIM), F32), jnp.zeros((B, RWKV_PROJ), F32),
        jnp.zeros((B, RET_HEADS, RET_DK, RET_DV), F32), 2, min(RWKV_CHUNK, T), W)

    past_len = page_table.shape[1] * cache_moba_k.shape[2]
    q_lo = RWKV_PROJ
    ys, ks, vs, rss, shs, rts = _run_trunk(
        x_sample.reshape(Bd * Td, D_MODEL), p_sample.reshape(DEPTH, Bd * Td, PLE_DIM), Bd, Td, past_len,
        lambda z, zk, zv: moba_sample(z[:, q_lo:q_lo + MOBA_WIDTH], zk, zv, cache_moba_k[0],
                                      cache_moba_v[0], page_table, Bd, Td),
        state_rwkv[0], state_rwkv_shift[0], state_ret[0], RWKV_HEADS, Td, W)

    return (yp, ys, kp, vp, ks, vs, rsp, rss, shp, shs, rtp, rts)
```

```python
import functools
import math

import numpy as np
import jax
import jax.numpy as jnp
from jax import lax
from jax.experimental import pallas as pl
from jax.experimental.pallas import tpu as pltpu

F32 = jnp.float32
BF16 = jnp.bfloat16
HI = lax.Precision.HIGHEST

D_MODEL = 1024
DEPTH = 2
PLE_DIM = 256
HEAD_DIM = 64
RWKV_HEADS = 8
RWKV_WIDTH = RWKV_HEADS * HEAD_DIM
W_LORA = 64
A_LORA = 64
G_LORA = 128
RWKV_PROJ = 3 * RWKV_WIDTH + W_LORA + A_LORA + G_LORA
RWKV_LN_EPS = 64e-5
RWKV_CHUNK = 64
RWKV_GROUP = 2
MOBA_HEADS = 8
MOBA_WIDTH = MOBA_HEADS * HEAD_DIM
MOBA_BLOCK = 256
MOBA_TOPK = 3
MOBA_QCHUNK = 128
MOBA_KEY_GROUP = 4 * MOBA_BLOCK
A_IN = RWKV_PROJ + 3 * MOBA_WIDTH
RET_HEADS = 4
RET_DK = 256
RET_DV = 512
RET_QK = RET_HEADS * RET_DK
RET_V = RET_HEADS * RET_DV
C_IN = 2 * RET_QK + 2 * RET_V
RET_CHUNK = 128
ROPE_BASE = 10000.0
D_FF = 2816
NORM_EPS = 1e-6
GN_EPS = 1e-5

LANE = 128
VMEM_LIMIT = 56 * 1024 * 1024

NN = (((1,), (0,)), ((), ()))
NT = (((1,), (1,)), ((), ()))
TN = (((0,), (0,)), ((), ()))


def _params(*sem):
    return pltpu.CompilerParams(dimension_semantics=sem, vmem_limit_bytes=VMEM_LIMIT)


def _rms(x, g):
    return x * lax.rsqrt(jnp.mean(x * x, axis=-1, keepdims=True) + NORM_EPS) * g


def _row_tile(n):
    for t in (1024, 512, 256, 128, 64, 32, 16, 8):
        if n % t == 0:
            return t
    raise ValueError(n)


def _norm_matmul_kernel(x_ref, g_ref, w_ref, o_ref, xn_ref):
    @pl.when(pl.program_id(1) == 0)
    def _():
        xn_ref[...] = _rms(x_ref[...], g_ref[...]).astype(BF16)

    o_ref[...] = jnp.dot(xn_ref[...], w_ref[...], preferred_element_type=F32)


def norm_matmul(x, g, w, tn):
    n, d = x.shape
    f = w.shape[1]
    tm = _row_tile(n)
    return pl.pallas_call(
        _norm_matmul_kernel,
        out_shape=jax.ShapeDtypeStruct((n, f), F32),
        grid=(n // tm, f // tn),
        in_specs=[pl.BlockSpec((tm, d), lambda i, j: (i, 0)),
                  pl.BlockSpec((1, d), lambda i, j: (0, 0)),
                  pl.BlockSpec((d, tn), lambda i, j: (0, j))],
        out_specs=pl.BlockSpec((tm, tn), lambda i, j: (i, j)),
        scratch_shapes=[pltpu.VMEM((tm, d), BF16)],
        compiler_params=_params("parallel", "arbitrary"),
        name="norm_matmul",
    )(x, g.reshape(1, d), w)


def _proj_res_kernel(*refs, n_x):
    h_ref = refs[0]
    xs = refs[1:1 + n_x]
    ws = refs[1 + n_x:1 + 2 * n_x]
    o_ref = refs[1 + 2 * n_x]
    acc = h_ref[...]
    for x_ref, w_ref in zip(xs, ws):
        acc = acc + jnp.dot(x_ref[...].astype(BF16), w_ref[...], preferred_element_type=F32)
    o_ref[...] = acc


def proj_residual(h, xs, ws):
    n, d = h.shape
    tm = min(_row_tile(n), 512)
    in_specs = [pl.BlockSpec((tm, d), lambda i: (i, 0))]
    in_specs += [pl.BlockSpec((tm, x.shape[1]), lambda i: (i, 0)) for x in xs]
    in_specs += [pl.BlockSpec(w.shape, lambda i: (0, 0)) for w in ws]
    return pl.pallas_call(
        functools.partial(_proj_res_kernel, n_x=len(xs)),
        out_shape=jax.ShapeDtypeStruct((n, d), F32),
        grid=(n // tm,),
        in_specs=in_specs,
        out_specs=pl.BlockSpec((tm, d), lambda i: (i, 0)),
        compiler_params=_params("parallel"),
        name="proj_residual",
    )(h, *xs, *ws)


def _ffn_kernel(h_ref, g_ref, wg_ref, wu_ref, wd_ref, o_ref, xn_ref):
    @pl.when(pl.program_id(1) == 0)
    def _():
        x = h_ref[...]
        xn_ref[...] = _rms(x, g_ref[...]).astype(BF16)
        o_ref[...] = x

    xn = xn_ref[...]
    a = jnp.dot(xn, wg_ref[...], preferred_element_type=F32)
    b = jnp.dot(xn, wu_ref[...], preferred_element_type=F32)
    t = (a * jax.nn.sigmoid(a)) * b
    o_ref[...] += jnp.dot(t.astype(BF16), wd_ref[...], preferred_element_type=F32)


def ffn_residual(h, g, wg, wu, wd, tf=256):
    n, d = h.shape
    ff = wg.shape[1]
    tm = _row_tile(n)
    return pl.pallas_call(
        _ffn_kernel,
        out_shape=jax.ShapeDtypeStruct((n, d), F32),
        grid=(n // tm, ff // tf),
        in_specs=[pl.BlockSpec((tm, d), lambda i, j: (i, 0)),
                  pl.BlockSpec((1, d), lambda i, j: (0, 0)),
                  pl.BlockSpec((d, tf), lambda i, j: (0, j)),
                  pl.BlockSpec((d, tf), lambda i, j: (0, j)),
                  pl.BlockSpec((tf, d), lambda i, j: (j, 0))],
        out_specs=pl.BlockSpec((tm, d), lambda i, j: (i, 0)),
        scratch_shapes=[pltpu.VMEM((tm, d), BF16)],
        compiler_params=_params("parallel", "arbitrary"),
        name="ffn_residual",
    )(h, g.reshape(1, d), wg, wu, wd)


def _ple_kernel(h_ref, g_ref, p_ref, wp_ref, wgate_ref, fn_ref, o_ref, *, final):
    x = h_ref[...]
    xn = _rms(x, g_ref[...]).astype(BF16)
    gate = jax.nn.sigmoid(jnp.dot(xn, wgate_ref[...], preferred_element_type=F32))
    pp = jnp.dot(p_ref[...].astype(BF16), wp_ref[...], preferred_element_type=F32)
    hn = x + pp * gate
    if final:
        hn = _rms(hn, fn_ref[...])
    o_ref[...] = hn


def ple_residual(h, g, p, wp, wgate, final_g, final):
    n, d = h.shape
    pd = p.shape[1]
    tm = min(_row_tile(n), 512)
    return pl.pallas_call(
        functools.partial(_ple_kernel, final=final),
        out_shape=jax.ShapeDtypeStruct((n, d), F32),
        grid=(n // tm,),
        in_specs=[pl.BlockSpec((tm, d), lambda i: (i, 0)),
                  pl.BlockSpec((1, d), lambda i: (0, 0)),
                  pl.BlockSpec((tm, pd), lambda i: (i, 0)),
                  pl.BlockSpec((pd, d), lambda i: (0, 0)),
                  pl.BlockSpec((d, d), lambda i: (0, 0)),
                  pl.BlockSpec((1, d), lambda i: (0, 0))],
        out_specs=pl.BlockSpec((tm, d), lambda i: (i, 0)),
        compiler_params=_params("parallel"),
        name="ple_residual",
    )(h, g.reshape(1, d), p, wp, wgate, final_g.reshape(1, d))


def _softplus(x):
    return jnp.maximum(x, 0.0) + jnp.log1p(jnp.exp(-jnp.abs(x)))


def _split2(x):
    hi = x.astype(BF16)
    return hi, (x - hi.astype(F32)).astype(BF16)


def _split3(x):
    hi = x.astype(BF16)
    r1 = x - hi.astype(F32)
    mid = r1.astype(BF16)
    return hi, mid, (r1 - mid.astype(F32)).astype(BF16)


def _dot3(a, b, dims=NN):
    ah, al = _split2(a)
    bh, bl = _split2(b)
    d = lambda x, y: lax.dot_general(x, y, dims, preferred_element_type=F32)
    return d(ah, bh) + (d(ah, bl) + d(al, bh))


def _dot_mask(a, m, mask_left=False):
    d = lambda x: lax.dot_general(m, x, NN, preferred_element_type=F32) if mask_left else \
        lax.dot_general(x, m, NN, preferred_element_type=F32)
    hi, mid, lo = _split3(a)
    return d(hi) + (d(mid) + d(lo))


def _rwkv_kernel(zr_ref, zk_ref, zv_ref, zl_ref, sr_ref, sk_ref, sv_ref, sl_ref,
                 mur_ref, muk_ref, muv_ref, mul_ref, w0_ref, wwa_ref, a0_ref, g2_ref,
                 kkw_ref, kaw_ref, rkw_ref, lnw_ref, lnb_ref, s0_ref,
                 hm_ref, mst_ref, min_ref, tri_ref, bd_ref,
                 out_ref, s_out_ref, lr_ref, lk_ref, lv_ref, ll_ref, sbd_ref, *, G, C):
    L = G * HEAD_DIM
    R = G * C
    NG = RWKV_HEADS // G
    c = pl.program_id(1)
    wide = R % LANE == 0

    @pl.when(c == 0)
    def _():
        lr_ref[...] = sr_ref[0]
        lk_ref[...] = sk_ref[0]
        lv_ref[...] = sv_ref[0]
        ll_ref[...] = sl_ref[0]
        bdm = bd_ref[:L, :L].astype(F32)
        for gi in range(NG):
            sc = s0_ref[0, :, gi * L:(gi + 1) * L]
            sbd_ref[gi] = jnp.concatenate([sc] * G, axis=0) * bdm

    def shift_mix(x_ref, last_ref, mu_ref):
        x = x_ref[...]
        row = lax.broadcasted_iota(jnp.int32, x.shape, 0)
        prev = jnp.where(row == 0, last_ref[...], pltpu.roll(x, 1, axis=0))
        last_ref[...] = x[C - 1:C, :]
        return x + mu_ref[...] * (prev - x)

    r = shift_mix(zr_ref, lr_ref, mur_ref)
    k = shift_mix(zk_ref, lk_ref, muk_ref)
    v = shift_mix(zv_ref, lv_ref, muv_ref)
    lo = shift_mix(zl_ref, ll_ref, mul_ref)

    wa_in = lo[:, :W_LORA + A_LORA]
    lane = lax.broadcasted_iota(jnp.int32, wa_in.shape, 1)
    wa_in = jnp.where(lane < W_LORA, jnp.tanh(wa_in), wa_in)
    wa = jnp.dot(wa_in, wwa_ref[...], preferred_element_type=F32)
    logw = -_softplus(-(w0_ref[...] + wa[:, :RWKV_WIDTH])) - 0.5
    ld = -jnp.exp(logw)
    a = jax.nn.sigmoid(a0_ref[...] + wa[:, RWKV_WIDTH:])
    g = jnp.dot(jax.nn.sigmoid(lo[:, W_LORA + A_LORA:]), g2_ref[...], preferred_element_type=F32)

    kkv = k * kkw_ref[...]
    k2 = k * (1.0 + (a - 1.0) * kaw_ref[...])
    sums = _dot_mask(jnp.concatenate([kkv * kkv, r * k2 * rkw_ref[...]], axis=0), bd_ref[...])
    kkn = kkv / jnp.maximum(jnp.sqrt(sums[:C]), 1e-12)
    bonus = sums[C:] * v
    an = -kkn
    bn = kkn * a

    cs = _dot_mask(ld, tri_ref[...], mask_left=True)
    e_neg = jnp.exp(-cs)
    e_pos = jnp.exp(cs)
    at = an * jnp.exp(cs - ld)
    rt = r * e_pos
    bt = bn * e_neg
    kt = k2 * e_neg
    pc = e_pos[C - 1:C, :]

    hm = hm_ref[...]
    strict = mst_ref[...] > 0.5
    incl = min_ref[...] > 0.5
    n_dbl = max(1, math.ceil(math.log2(C)))
    groups = range(NG)
    sls = [slice(gi * L, (gi + 1) * L) for gi in groups]
    stack = lambda x, sl: jnp.concatenate([x[:, sl]] * G, axis=0) * hm
    a_s = [stack(at, sl) for sl in sls]
    r_s = [stack(rt, sl) for sl in sls]
    b_s = [stack(bt, sl) for sl in sls]
    k_s = [stack(kt, sl) for sl in sls]
    v_s = [stack(v, sl) for sl in sls]
    lhs = [jnp.concatenate([a_s[i], r_s[i]], axis=0) for i in groups]
    bk = [jnp.concatenate([b_s[i], k_s[i]], axis=0) for i in groups]
    if wide:
        m = [_dot3(lhs[i], bk[i], NT) for i in groups]
        m_b = [x[:, :R] for x in m]
        m_k = [x[:, R:] for x in m]
    else:
        m_b = [_dot3(lhs[i], b_s[i], NT) for i in groups]
        m_k = [_dot3(lhs[i], k_s[i], NT) for i in groups]
    a_ab = [jnp.where(strict, x[:R], 0.0) for x in m_b]
    a_ak = [jnp.where(strict, x[:R], 0.0) for x in m_k]
    a_rb = [jnp.where(incl, x[R:], 0.0) for x in m_b]
    a_rk = [jnp.where(incl, x[R:], 0.0) for x in m_k]

    sb = [sbd_ref[i] for i in groups]
    w0s = [_dot3(lhs[i], sb[i], NT) for i in groups]
    u = [w0s[i][:R] + _dot3(a_ak[i], v_s[i]) for i in groups]
    ap = a_ab
    for j in range(n_dbl):
        if j + 1 == n_dbl:
            u = [u[i] + _dot3(ap[i], u[i]) for i in groups]
        elif wide:
            t = [_dot3(ap[i], jnp.concatenate([u[i], ap[i]], axis=1)) for i in groups]
            u = [u[i] + t[i][:, :L] for i in groups]
            ap = [t[i][:, L:] for i in groups]
        else:
            u, ap = ([u[i] + _dot3(ap[i], u[i]) for i in groups], [_dot3(ap[i], ap[i]) for i in groups])
    if wide:
        y = [w0s[i][R:] + _dot3(jnp.concatenate([a_rb[i], a_rk[i]], axis=1),
                                jnp.concatenate([u[i], v_s[i]], axis=0)) for i in groups]
    else:
        y = [w0s[i][R:] + _dot3(a_rb[i], u[i]) + _dot3(a_rk[i], v_s[i]) for i in groups]

    ys = []
    for i in groups:
        mu_y = jnp.sum(y[i], axis=-1, keepdims=True) * (1.0 / HEAD_DIM)
        yc = (y[i] - mu_y) * hm
        yn = yc * lax.rsqrt(jnp.sum(yc * yc, axis=-1, keepdims=True) * (1.0 / HEAD_DIM) + RWKV_LN_EPS)
        y_cl = yn[:C]
        for h in range(1, G):
            y_cl = y_cl + yn[h * C:(h + 1) * C]
        ys.append(y_cl)

    uv = [jnp.concatenate([u[i], v_s[i]], axis=0) for i in groups]
    s_new = [(sb[i] + _dot3(uv[i], bk[i], TN)) * pc[:, sls[i]] for i in groups]
    for i in groups:
        sbd_ref[i] = s_new[i]
        s_cat = s_new[i][:HEAD_DIM]
        for h in range(1, G):
            s_cat = s_cat + s_new[i][h * HEAD_DIM:(h + 1) * HEAD_DIM]
        s_out_ref[0, :, sls[i]] = s_cat
    y_all = ys[0] if NG == 1 else jnp.concatenate(ys, axis=1)
    out_ref[...] = (y_all * lnw_ref[...] + lnb_ref[...] + bonus) * g


def _rwkv_tables(G, C):
    L, R = G * HEAD_DIM, G * C
    i = np.arange(R)[:, None]
    j = np.arange(R)[None, :]
    l = np.arange(RWKV_WIDTH)
    same = (i // C) == (j // C)
    hm = ((np.arange(R)[:, None] // C) == (np.arange(L)[None, :] // HEAD_DIM))
    tri = np.arange(C)[:, None] >= np.arange(C)[None, :]
    bd = (l[:, None] // HEAD_DIM) == (l[None, :] // HEAD_DIM)
    f = lambda m, dt=np.float32: jnp.asarray(m.astype(np.float32), dtype=dt)
    return [f(hm), f(same & (i > j)), f(same & (i >= j)), f(tri, BF16), f(bd, BF16)]


def rwkv_mix(z, shift0, s0_cat, prm, B, T, G, C):
    L = G * HEAD_DIM
    NG = RWKV_HEADS // G
    NC = T // C
    W = RWKV_WIDTH
    lw = W_LORA + A_LORA + G_LORA
    lb = (3 * W) // lw
    row = lambda b, c: b * NC + c
    zspec = lambda off: pl.BlockSpec((C, W), lambda b, c: (row(b, c), off))
    sspec = lambda off: pl.BlockSpec((1, 1, W), lambda b, c: (b, 0, off))
    vspec = lambda off: pl.BlockSpec((1, W), lambda b, c: (0, off))
    const = lambda a: pl.BlockSpec(a.shape, lambda b, c: (0,) * a.ndim)
    tables = _rwkv_tables(G, C)
    mu = prm["mu"].reshape(1, RWKV_PROJ)
    sh = shift0.reshape(B, 1, RWKV_PROJ)
    in_specs = [zspec(0), zspec(1), zspec(2),
                pl.BlockSpec((C, lw), lambda b, c: (row(b, c), lb)),
                sspec(0), sspec(1), sspec(2),
                pl.BlockSpec((1, 1, lw), lambda b, c: (b, 0, lb)),
                vspec(0), vspec(1), vspec(2),
                pl.BlockSpec((1, lw), lambda b, c: (0, lb)),
                vspec(0),
                const(prm["wwa"]),
                vspec(0),
                const(prm["g2"]),
                vspec(0), vspec(0), vspec(0), vspec(0), vspec(0),
                pl.BlockSpec((1, HEAD_DIM, W), lambda b, c: (b, 0, 0))]
    in_specs += [const(t) for t in tables]
    out, s_t = pl.pallas_call(
        functools.partial(_rwkv_kernel, G=G, C=C),
        out_shape=(jax.ShapeDtypeStruct((B * T, W), F32),
                   jax.ShapeDtypeStruct((B, HEAD_DIM, W), F32)),
        grid=(B, NC),
        in_specs=in_specs,
        out_specs=(pl.BlockSpec((C, W), lambda b, c: (row(b, c), 0)),
                   pl.BlockSpec((1, HEAD_DIM, W), lambda b, c: (b, 0, 0))),
        scratch_shapes=[pltpu.VMEM((1, W), F32), pltpu.VMEM((1, W), F32),
                        pltpu.VMEM((1, W), F32), pltpu.VMEM((1, lw), F32),
                        pltpu.VMEM((NG, L, L), F32)],
        compiler_params=_params("parallel", "arbitrary"),
        name="rwkv_mix",
    )(z, z, z, z, sh, sh, sh, sh, mu, mu, mu, mu,
      prm["w0"], prm["wwa"], prm["a0"], prm["g2"], prm["k_k"], prm["k_a"], prm["r_k"],
      prm["ln_w"], prm["ln_b"], s0_cat, *tables)
    return out, s_t


def _pack_wwa(w2, a2):
    top = jnp.concatenate([w2, jnp.zeros_like(w2)], axis=1)
    bot = jnp.concatenate([jnp.zeros_like(a2), a2], axis=1)
    return jnp.concatenate([top, bot], axis=0)


def _top_blocks(gate, n_iota, n_valid, n_blocks, axis=-1):
    neg = -jnp.inf
    valid = n_iota < n_valid
    g = jnp.where(valid, gate, neg)
    sel = jnp.zeros(gate.shape, F32)
    for _ in range(MOBA_TOPK):
        m = jnp.max(g, axis=axis, keepdims=True)
        idx = jnp.min(jnp.where(g == m, n_iota, n_blocks), axis=axis, keepdims=True)
        pick = n_iota == idx
        sel = jnp.where(jnp.logical_and(pick, valid), 1.0, sel)
        g = jnp.where(pick, neg, g)
    return sel


def _moba_prompt_kernel(q_ref, k_ref, v_ref, o_ref, kbf_ref, vt_ref, kmean_ref, *, T):
    NB = T // MOBA_BLOCK
    QB = MOBA_BLOCK
    KG = MOBA_KEY_GROUP
    BPG = KG // MOBA_BLOCK
    NH = LANE // HEAD_DIM
    blk = pl.program_id(2)
    scale = HEAD_DIM ** -0.5

    @pl.when(blk == 0)
    def _():
        kf = k_ref[...]
        kbf_ref[...] = kf.astype(BF16)
        kmean_ref[...] = jnp.sum(kf.reshape(NB, MOBA_BLOCK, LANE), axis=1) * (1.0 / MOBA_BLOCK)
        for g in range(T // KG):
            vt_ref[g] = v_ref[g * KG:(g + 1) * KG, :].T.astype(BF16)

    q = q_ref[...]
    lane = lax.broadcasted_iota(jnp.int32, (QB, LANE), 1)
    n_iota = lax.broadcasted_iota(jnp.int32, (NB, QB), 0)
    kmean = kmean_ref[...]
    qbs, sels = [], []
    for hh in range(NH):
        qm = jnp.where((lane // HEAD_DIM) == hh, q, 0.0)
        gate = lax.dot_general(kmean, qm, NT, precision=HI, preferred_element_type=F32)
        qbs.append((qm * scale).astype(BF16))
        sels.append(_top_blocks(gate, n_iota, blk, NB, axis=0))

    kidx = lax.broadcasted_iota(jnp.int32, (MOBA_BLOCK, QB), 0)
    qidx = lax.broadcasted_iota(jnp.int32, (MOBA_BLOCK, QB), 1)
    n_groups = (blk * QB) // KG + 1

    def step(gi, carry, own_group):
        st = pl.multiple_of(gi * KG, KG)
        kg = kbf_ref[pl.ds(st, KG), :]
        vtg = vt_ref[gi]
        heads = range(NH)
        s = [lax.dot_general(kg, qbs[hh], NT, preferred_element_type=F32) for hh in heads]

        def mask(hh):
            parts = []
            for j in range(BPG):
                n = gi * BPG + j
                ok = jnp.max(jnp.where(n_iota == n, sels[hh], 0.0), axis=0, keepdims=True) > 0.5
                if own_group:
                    ok = jnp.logical_or(ok, jnp.logical_and(n == blk, kidx <= qidx))
                parts.append(jnp.where(ok, s[hh][j * MOBA_BLOCK:(j + 1) * MOBA_BLOCK], -jnp.inf))
            return jnp.concatenate(parts, axis=0)

        s = [mask(hh) for hh in heads]
        m_new = [jnp.maximum(carry[hh][0], jnp.max(s[hh], axis=0, keepdims=True)) for hh in heads]
        alpha = [jnp.exp(carry[hh][0] - m_new[hh]) for hh in heads]
        p = [jnp.exp(s[hh] - m_new[hh]) for hh in heads]
        l = [alpha[hh] * carry[hh][1] + jnp.sum(p[hh], axis=0, keepdims=True) for hh in heads]
        pv = [jnp.dot(vtg, p[hh].astype(BF16), preferred_element_type=F32) for hh in heads]
        return tuple((m_new[hh], l[hh], alpha[hh] * carry[hh][2] + pv[hh]) for hh in heads)

    init = tuple((jnp.full((1, QB), -jnp.inf, F32), jnp.zeros((1, QB), F32), jnp.zeros((LANE, QB), F32))
                 for _ in range(NH))
    first = step(n_groups - 1, init, True)
    res = lax.fori_loop(1, n_groups, lambda i, carry: step(n_groups - 1 - i, carry, False), first)
    rowi = lax.broadcasted_iota(jnp.int32, (LANE, QB), 0)
    out_t = jnp.zeros((LANE, QB), F32)
    for hh in range(NH):
        _, l, acc = res[hh]
        out_t = jnp.where((rowi // HEAD_DIM) == hh, acc / l, out_t)
    o_ref[...] = out_t.T


def moba_prompt(z, B, T):
    assert T % MOBA_KEY_GROUP == 0 and MOBA_BLOCK % MOBA_QCHUNK == 0
    NB = T // MOBA_BLOCK
    NKG = T // MOBA_KEY_GROUP
    P = MOBA_WIDTH // LANE
    q0 = RWKV_PROJ // LANE
    k0 = (RWKV_PROJ + MOBA_WIDTH) // LANE
    v0 = (RWKV_PROJ + 2 * MOBA_WIDTH) // LANE
    return pl.pallas_call(
        functools.partial(_moba_prompt_kernel, T=T),
        out_shape=jax.ShapeDtypeStruct((B * T, MOBA_WIDTH), F32),
        grid=(B, P, NB),
        in_specs=[pl.BlockSpec((MOBA_BLOCK, LANE), lambda b, p, c: (b * NB + c, q0 + p)),
                  pl.BlockSpec((T, LANE), lambda b, p, c: (b, k0 + p)),
                  pl.BlockSpec((T, LANE), lambda b, p, c: (b, v0 + p))],
        out_specs=pl.BlockSpec((MOBA_BLOCK, LANE), lambda b, p, c: (b * NB + c, p)),
        scratch_shapes=[pltpu.VMEM((T, LANE), BF16), pltpu.VMEM((NKG, LANE, MOBA_KEY_GROUP), BF16),
                        pltpu.VMEM((NB, LANE), F32)],
        compiler_params=_params("parallel", "parallel", "arbitrary"),
        name="moba_prompt",
    )(z, z, z)


def _moba_sample_kernel(pt_ref, q_ref, kn_ref, vn_ref, *refs, n_pages, tq):
    kp = refs[:n_pages]
    vp = refs[n_pages:2 * n_pages]
    hm_ref = refs[2 * n_pages]
    o_ref = refs[2 * n_pages + 1]
    H = MOBA_HEADS
    R = H * tq
    page = kp[0].shape[3]
    ppb = MOBA_BLOCK // page
    n_past = n_pages // ppb
    scale = HEAD_DIM ** -0.5
    hm = hm_ref[...]
    qs = jnp.concatenate([q_ref[...]] * H, axis=0) * (hm * scale)
    qh, ql = _split2(qs)

    n_iota = lax.broadcasted_iota(jnp.int32, (R, n_past), 1)
    pages = range(n_pages)
    ks = [_split2(kp[j][0].reshape(MOBA_WIDTH, page)) for j in pages]
    s_hi = [jnp.dot(qh, ks[j][0], preferred_element_type=F32) for j in pages]
    s_x1 = [jnp.dot(qh, ks[j][1], preferred_element_type=F32) for j in pages]
    s_x2 = [jnp.dot(ql, ks[j][0], preferred_element_type=F32) for j in pages]
    tot = [jnp.sum(s_hi[j] + (s_x1[j] + s_x2[j]), axis=-1, keepdims=True) for j in pages]
    gate = jnp.zeros((R, n_past), F32)
    for n in range(n_past):
        blk_tot = tot[n * ppb]
        for j in range(1, ppb):
            blk_tot = blk_tot + tot[n * ppb + j]
        gate = jnp.where(n_iota == n, blk_tot, gate)
    sel = _top_blocks(gate, n_iota, n_past, n_past + 1)

    scores = [jnp.where(sel[:, j // ppb:j // ppb + 1] > 0.5, s_hi[j], -jnp.inf) for j in range(n_pages)]
    s_own = lax.dot_general(qh, kn_ref[...].astype(BF16), NT, preferred_element_type=F32)
    qi = lax.broadcasted_iota(jnp.int32, (R, tq), 0) % tq
    ki = lax.broadcasted_iota(jnp.int32, (R, tq), 1)
    s_own = jnp.where(ki <= qi, s_own, -jnp.inf)

    m = jnp.max(s_own, axis=-1, keepdims=True)
    for s in scores:
        m = jnp.maximum(m, jnp.max(s, axis=-1, keepdims=True))
    p_own = jnp.exp(s_own - m)
    l = jnp.sum(p_own, axis=-1, keepdims=True)
    acc = jnp.dot(p_own.astype(BF16), vn_ref[...].astype(BF16), preferred_element_type=F32)
    p = [jnp.exp(scores[j] - m) for j in pages]
    vt = [vp[j][0].reshape(MOBA_WIDTH, page).astype(BF16) for j in pages]
    pv = [lax.dot_general(p[j].astype(BF16), vt[j], NT, preferred_element_type=F32) for j in pages]
    for j in pages:
        l = l + jnp.sum(p[j], axis=-1, keepdims=True)
        acc = acc + pv[j]
    om = (acc / l) * hm
    out = om[:tq]
    for h in range(1, H):
        out = out + om[h * tq:(h + 1) * tq]
    o_ref[...] = out


def moba_sample(zq, zk, zv, pool_k, pool_v, page_table, B, T):
    n_pages = page_table.shape[1]
    n_pool, page_size = pool_k.shape[0], pool_k.shape[1]
    assert MOBA_BLOCK % page_size == 0 and (n_pages * page_size) % MOBA_BLOCK == 0
    assert T <= MOBA_BLOCK and T % MOBA_QCHUNK != 0 and n_pages * page_size // MOBA_BLOCK >= MOBA_TOPK
    H = MOBA_HEADS
    R = H * T
    pk = pool_k.transpose(0, 2, 3, 1)
    pv = pool_v.transpose(0, 2, 3, 1)
    hm = jnp.asarray(((np.arange(R)[:, None] // T) == (np.arange(MOBA_WIDTH)[None, :] // HEAD_DIM))
                     .astype(np.float32))
    rows = pl.BlockSpec((T, MOBA_WIDTH), lambda b, pt: (b, 0))
    page = lambda j: pl.BlockSpec((1, H, HEAD_DIM, page_size), lambda b, pt: (pt[b, j], 0, 0, 0))
    return pl.pallas_call(
        functools.partial(_moba_sample_kernel, n_pages=n_pages, tq=T),
        out_shape=jax.ShapeDtypeStruct((B * T, MOBA_WIDTH), F32),
        grid_spec=pltpu.PrefetchScalarGridSpec(
            num_scalar_prefetch=1,
            grid=(B,),
            in_specs=[rows, rows, rows] + [page(j) for j in range(n_pages)] * 2
            + [pl.BlockSpec(hm.shape, lambda b, pt: (0, 0))],
            out_specs=rows),
        compiler_params=_params("parallel"),
        name="moba_sample",
    )(page_table, zq, zk, zv, *([pk] * n_pages), *([pv] * n_pages), hm)


def _retention_kernel(q_ref, k_ref, v_ref, g_ref, cos_ref, sin_ref, lg_ref, gnw_ref, s0_ref,
                      o_ref, s_out_ref, *, C):
    c = pl.program_id(1)

    @pl.when(c == 0)
    def _():
        s_out_ref[...] = s0_ref[...]

    half = RET_DK // 2
    heads = range(RET_HEADS)
    cos = cos_ref[...]
    sin = sin_ref[...]

    def rot(x_ref, h):
        x1 = x_ref[:, h * RET_DK:h * RET_DK + half]
        x2 = x_ref[:, h * RET_DK + half:(h + 1) * RET_DK]
        return jnp.concatenate([x1 * cos - x2 * sin, x1 * sin + x2 * cos], axis=-1)

    ii = lax.broadcasted_iota(jnp.int32, (C, C), 0)
    jj = lax.broadcasted_iota(jnp.int32, (C, C), 1)
    diff = jnp.maximum((ii - jj).astype(F32), 0.0)
    it = lax.broadcasted_iota(jnp.int32, (C, LANE), 0).astype(F32)
    lg = [lg_ref[h] for h in heads]
    dmask = [jnp.where(ii >= jj, jnp.exp(diff * lg[h][:, :C]), 0.0) for h in heads]
    cross = [jnp.exp((it + 1.0) * lg[h])[:, :1] for h in heads]
    kdec = [jnp.exp((C - 1.0 - it) * lg[h])[:, :1] for h in heads]
    sdec = [jnp.exp(C * lg[h])[:, :1] for h in heads]

    qb = [rot(q_ref, h).astype(BF16) for h in heads]
    kr = [rot(k_ref, h) * (RET_DK ** -0.5) for h in heads]
    v = [v_ref[:, h * RET_DV:(h + 1) * RET_DV] for h in heads]
    vb = [x.astype(BF16) for x in v]
    s = [s_out_ref[0, h] for h in heads]
    att = [lax.dot_general(qb[h], kr[h].astype(BF16), NT, preferred_element_type=F32) * dmask[h]
           for h in heads]
    y_in = [jnp.dot(att[h].astype(BF16), vb[h], preferred_element_type=F32) for h in heads]
    y_x = [jnp.dot(qb[h], s[h].astype(BF16), preferred_element_type=F32) * cross[h] for h in heads]
    for h in heads:
        s_out_ref[0, h] = s[h] * sdec[h] + lax.dot_general(kr[h] * kdec[h], v[h], TN,
                                                           preferred_element_type=F32)
    for h in heads:
        y = y_in[h] + y_x[h]
        mu = jnp.mean(y, axis=-1, keepdims=True)
        yc = y - mu
        sl = slice(h * RET_DV, (h + 1) * RET_DV)
        yn = yc * lax.rsqrt(jnp.mean(yc * yc, axis=-1, keepdims=True) + GN_EPS) * gnw_ref[:, sl]
        g = g_ref[:, sl]
        o_ref[:, sl] = (g * jax.nn.sigmoid(g)) * yn


def retention_mix(z, s0, gn_w, B, T, pos0):
    C = RET_CHUNK if T % RET_CHUNK == 0 else T
    NC = T // C
    half = RET_DK // 2
    pos = (pos0 + jnp.arange(T)).astype(F32)
    inv = ROPE_BASE ** (-jnp.arange(half, dtype=F32) / half)
    ang = pos[:, None] * inv[None, :]
    log_g = jnp.log1p(-jnp.exp2(-5.0 - jnp.arange(RET_HEADS, dtype=F32)))
    lg = jnp.broadcast_to(log_g[:, None, None], (RET_HEADS, 1, LANE))
    row = lambda b, c: b * NC + c
    state = pl.BlockSpec((1, RET_HEADS, RET_DK, RET_DV), lambda b, c: (b, 0, 0, 0))
    return pl.pallas_call(
        functools.partial(_retention_kernel, C=C),
        out_shape=(jax.ShapeDtypeStruct((B * T, RET_V), F32),
                   jax.ShapeDtypeStruct((B, RET_HEADS, RET_DK, RET_DV), F32)),
        grid=(B, NC),
        in_specs=[pl.BlockSpec((C, RET_QK), lambda b, c: (row(b, c), 0)),
                  pl.BlockSpec((C, RET_QK), lambda b, c: (row(b, c), 1)),
                  pl.BlockSpec((C, RET_V), lambda b, c: (row(b, c), 2 * RET_QK // RET_V)),
                  pl.BlockSpec((C, RET_V), lambda b, c: (row(b, c), 2 * RET_QK // RET_V + 1)),
                  pl.BlockSpec((C, half), lambda b, c: (c, 0)),
                  pl.BlockSpec((C, half), lambda b, c: (c, 0)),
                  pl.BlockSpec((RET_HEADS, 1, LANE), lambda b, c: (0, 0, 0)),
                  pl.BlockSpec((1, RET_V), lambda b, c: (0, 0)),
                  state],
        out_specs=(pl.BlockSpec((C, RET_V), lambda b, c: (row(b, c), 0)), state),
        compiler_params=_params("parallel", "arbitrary"),
        name="retention_mix",
    )(z, z, z, z, jnp.cos(ang), jnp.sin(ang), lg, gn_w.reshape(1, RET_V), s0)


def _run_trunk(x, p, B, T, pos0, moba_fn, rwkv_s0, rwkv_shift0, ret_s0, G, C, W):
    h = x
    z = norm_matmul(h, W["norm_mix"][0], W["a_w_in"][0], tn=A_IN // 2)
    s0_cat = rwkv_s0.transpose(0, 2, 1, 3).reshape(B, HEAD_DIM, RWKV_WIDTH)
    rwkv_out, s_cat = rwkv_mix(z, rwkv_shift0, s0_cat, W["rwkv"], B, T, G, C)
    rwkv_state = s_cat.reshape(B, HEAD_DIM, RWKV_HEADS, HEAD_DIM).transpose(0, 2, 1, 3)
    zk = z[:, RWKV_PROJ + MOBA_WIDTH:RWKV_PROJ + 2 * MOBA_WIDTH]
    zv = z[:, RWKV_PROJ + 2 * MOBA_WIDTH:]
    moba_out = moba_fn(z, zk, zv)
    shift_t = z.reshape(B, T, A_IN)[:, T - 1, :RWKV_PROJ]
    wo = W["a_w_out"][0]
    h = proj_residual(h, [rwkv_out, moba_out], [wo[:RWKV_WIDTH], wo[RWKV_WIDTH:]])
    h = ffn_residual(h, W["norm_ffn"][0], W["ffn_w_gate"][0], W["ffn_w_up"][0], W["ffn_w_down"][0])
    h = ple_residual(h, W["ple_norm"][0], p[0], W["ple_proj"][0], W["ple_gate"][0],
                     W["final_norm"], final=False)
    z2 = norm_matmul(h, W["norm_mix"][1], W["c_w_in"][0], tn=C_IN // 3)
    gated, ret_state = retention_mix(z2, ret_s0, W["ret_gn_w"][0], B, T, pos0)
    h = proj_residual(h, [gated], [W["c_w_out"][0]])
    h = ffn_residual(h, W["norm_ffn"][1], W["ffn_w_gate"][1], W["ffn_w_up"][1], W["ffn_w_down"][1])
    y = ple_residual(h, W["ple_norm"][1], p[1], W["ple_proj"][1], W["ple_gate"][1],
                     W["final_norm"], final=True)
    k_rows = zk.reshape(1, B, T, MOBA_HEADS, HEAD_DIM)
    v_rows = zv.reshape(1, B, T, MOBA_HEADS, HEAD_DIM)
    return (y.reshape(B, T, D_MODEL), k_rows, v_rows, rwkv_state[None],
            shift_t[None], ret_state[None])


def kernel(x_prompt, x_sample, cache_moba_k, cache_moba_v, state_rwkv, state_rwkv_shift, state_ret, page_table, p_prompt, p_sample, norm_mix, norm_ffn, ffn_w_gate, ffn_w_up, ffn_w_down, ple_norm, ple_gate, ple_proj, a_w_in, rwkv_mu, rwkv_w0, rwkv_w2, rwkv_a0, rwkv_a2, rwkv_g2, rwkv_k_k, rwkv_k_a, rwkv_r_k, rwkv_ln_w, rwkv_ln_b, a_w_out, c_w_in, ret_gn_w, c_w_out, final_norm):
    assert norm_mix.shape[0] == DEPTH == 2
    B, T, _ = x_prompt.shape
    Bd, Td, _ = x_sample.shape
    bf = lambda w: w.astype(BF16)
    row = lambda a: a.reshape(1, RWKV_WIDTH)
    rwkv = dict(mu=rwkv_mu[0], w0=row(rwkv_w0[0]), a0=row(rwkv_a0[0]), g2=rwkv_g2[0],
                k_k=row(rwkv_k_k[0]), k_a=row(rwkv_k_a[0]), r_k=row(rwkv_r_k[0]),
                ln_w=row(rwkv_ln_w[0]), ln_b=row(rwkv_ln_b[0]),
                wwa=_pack_wwa(rwkv_w2[0], rwkv_a2[0]))
    W = dict(norm_mix=norm_mix, norm_ffn=norm_ffn, ffn_w_gate=bf(ffn_w_gate), ffn_w_up=bf(ffn_w_up),
             ffn_w_down=bf(ffn_w_down), ple_norm=ple_norm, ple_gate=bf(ple_gate), ple_proj=bf(ple_proj),
             a_w_in=bf(a_w_in), a_w_out=bf(a_w_out), c_w_in=bf(c_w_in), c_w_out=bf(c_w_out),
             ret_gn_w=ret_gn_w, final_norm=final_norm, rwkv=rwkv)

    yp, kp, vp, rsp, shp, rtp = _run_trunk(
        x_prompt.reshape(B * T, D_MODEL), p_prompt.reshape(DEPTH, B * T, PLE_DIM), B, T, 0,
        lambda z, zk, zv: moba_prompt(z, B, T),
        jnp.zeros((B, RWKV_HEADS, HEAD_DIM, HEAD_DIM), F32), jnp.zeros((B, RWKV_PROJ), F32),
        jnp.zeros((B, RET_HEADS, RET_DK, RET_DV), F32), RWKV_GROUP, min(RWKV_CHUNK, T), W)

    past_len = page_table.shape[1] * cache_moba_k.shape[2]
    q_lo = RWKV_PROJ
    ys, ks, vs, rss, shs, rts = _run_trunk(
        x_sample.reshape(Bd * Td, D_MODEL), p_sample.reshape(DEPTH, Bd * Td, PLE_DIM), Bd, Td, past_len,
        lambda z, zk, zv: moba_sample(z[:, q_lo:q_lo + MOBA_WIDTH], zk, zv, cache_moba_k[0],
                                      cache_moba_v[0], page_table, Bd, Td),
        state_rwkv[0], state_rwkv_shift[0], state_ret[0], RWKV_GROUP, min(RWKV_CHUNK, Td), W)

    return (yp, ys, kp, vp, ks, vs, rsp, rss, shp, shs, rtp, rts)
```

```python
import functools
import math

import numpy as np
import jax
import jax.numpy as jnp
from jax import lax
from jax.experimental import pallas as pl
from jax.experimental.pallas import tpu as pltpu

F32 = jnp.float32
BF16 = jnp.bfloat16
HI = lax.Precision.HIGHEST

D_MODEL = 1024
DEPTH = 2
PLE_DIM = 256
HEAD_DIM = 64
RWKV_HEADS = 8
RWKV_WIDTH = RWKV_HEADS * HEAD_DIM
W_LORA = 64
A_LORA = 64
G_LORA = 128
RWKV_PROJ = 3 * RWKV_WIDTH + W_LORA + A_LORA + G_LORA
RWKV_LN_EPS = 64e-5
RWKV_CHUNK = 64
RWKV_GROUP = 2
RWKV_SOLVE_BLOCK = 8
MOBA_HEADS = 8
MOBA_WIDTH = MOBA_HEADS * HEAD_DIM
MOBA_BLOCK = 256
MOBA_TOPK = 3
MOBA_QCHUNK = 128
MOBA_KEY_GROUP = 4 * MOBA_BLOCK
A_IN = RWKV_PROJ + 3 * MOBA_WIDTH
RET_HEADS = 4
RET_DK = 256
RET_DV = 512
RET_QK = RET_HEADS * RET_DK
RET_V = RET_HEADS * RET_DV
C_IN = 2 * RET_QK + 2 * RET_V
RET_CHUNK = 128
ROPE_BASE = 10000.0
D_FF = 2816
NORM_EPS = 1e-6
GN_EPS = 1e-5
LOG2E = math.log2(math.e)

LANE = 128
VMEM_LIMIT = 56 * 1024 * 1024

NN = (((1,), (0,)), ((), ()))
NT = (((1,), (1,)), ((), ()))
TN = (((0,), (0,)), ((), ()))


def _params(*sem):
    return pltpu.CompilerParams(dimension_semantics=sem, vmem_limit_bytes=VMEM_LIMIT)


def _rms(x, g):
    return x * lax.rsqrt(jnp.mean(x * x, axis=-1, keepdims=True) + NORM_EPS) * g


def _row_tile(n):
    for t in (1024, 512, 256, 128, 64, 32, 16, 8):
        if n % t == 0:
            return t
    raise ValueError(n)


def _norm_matmul_kernel(x_ref, g_ref, w_ref, o_ref, xn_ref):
    @pl.when(pl.program_id(1) == 0)
    def _():
        xn_ref[...] = _rms(x_ref[...], g_ref[...]).astype(BF16)

    o_ref[...] = jnp.dot(xn_ref[...], w_ref[...], preferred_element_type=F32)


def norm_matmul(x, g, w, tn):
    n, d = x.shape
    f = w.shape[1]
    tm = _row_tile(n)
    return pl.pallas_call(
        _norm_matmul_kernel,
        out_shape=jax.ShapeDtypeStruct((n, f), F32),
        grid=(n // tm, f // tn),
        in_specs=[pl.BlockSpec((tm, d), lambda i, j: (i, 0)),
                  pl.BlockSpec((1, d), lambda i, j: (0, 0)),
                  pl.BlockSpec((d, tn), lambda i, j: (0, j))],
        out_specs=pl.BlockSpec((tm, tn), lambda i, j: (i, j)),
        scratch_shapes=[pltpu.VMEM((tm, d), BF16)],
        compiler_params=_params("parallel", "arbitrary"),
        name="norm_matmul",
    )(x, g.reshape(1, d), w)


def _proj_res_kernel(*refs, n_x):
    h_ref = refs[0]
    xs = refs[1:1 + n_x]
    ws = refs[1 + n_x:1 + 2 * n_x]
    o_ref = refs[1 + 2 * n_x]
    acc = h_ref[...]
    for x_ref, w_ref in zip(xs, ws):
        acc = acc + jnp.dot(x_ref[...].astype(BF16), w_ref[...], preferred_element_type=F32)
    o_ref[...] = acc


def proj_residual(h, xs, ws):
    n, d = h.shape
    tm = min(_row_tile(n), 512)
    in_specs = [pl.BlockSpec((tm, d), lambda i: (i, 0))]
    in_specs += [pl.BlockSpec((tm, x.shape[1]), lambda i: (i, 0)) for x in xs]
    in_specs += [pl.BlockSpec(w.shape, lambda i: (0, 0)) for w in ws]
    return pl.pallas_call(
        functools.partial(_proj_res_kernel, n_x=len(xs)),
        out_shape=jax.ShapeDtypeStruct((n, d), F32),
        grid=(n // tm,),
        in_specs=in_specs,
        out_specs=pl.BlockSpec((tm, d), lambda i: (i, 0)),
        compiler_params=_params("parallel"),
        name="proj_residual",
    )(h, *xs, *ws)


FFN_TILE = 256


def _ffn_kernel(h_ref, g_ref, wgu_ref, wd_ref, o_ref, xn_ref):
    @pl.when(pl.program_id(1) == 0)
    def _():
        x = h_ref[...]
        xn_ref[...] = _rms(x, g_ref[...]).astype(BF16)
        o_ref[...] = x

    ab = jnp.dot(xn_ref[...], wgu_ref[...], preferred_element_type=F32)
    a, b = ab[:, :FFN_TILE], ab[:, FFN_TILE:]
    t = (a * jax.nn.sigmoid(a)) * b
    o_ref[...] += jnp.dot(t.astype(BF16), wd_ref[...], preferred_element_type=F32)


def _pack_gate_up(wg, wu):
    d, ff = wg.shape
    nf = ff // FFN_TILE
    return jnp.stack([wg.reshape(d, nf, FFN_TILE), wu.reshape(d, nf, FFN_TILE)], axis=2).reshape(d, 2 * ff)


def ffn_residual(h, g, wgu, wd):
    n, d = h.shape
    ff = wd.shape[0]
    tf = FFN_TILE
    tm = _row_tile(n)
    return pl.pallas_call(
        _ffn_kernel,
        out_shape=jax.ShapeDtypeStruct((n, d), F32),
        grid=(n // tm, ff // tf),
        in_specs=[pl.BlockSpec((tm, d), lambda i, j: (i, 0)),
                  pl.BlockSpec((1, d), lambda i, j: (0, 0)),
                  pl.BlockSpec((d, 2 * tf), lambda i, j: (0, j)),
                  pl.BlockSpec((tf, d), lambda i, j: (j, 0))],
        out_specs=pl.BlockSpec((tm, d), lambda i, j: (i, 0)),
        scratch_shapes=[pltpu.VMEM((tm, d), BF16)],
        compiler_params=_params("parallel", "arbitrary"),
        name="ffn_residual",
    )(h, g.reshape(1, d), wgu, wd)


def _ple_kernel(h_ref, g_ref, p_ref, wp_ref, wgate_ref, fn_ref, o_ref, *, final):
    x = h_ref[...]
    xn = _rms(x, g_ref[...]).astype(BF16)
    gate = jax.nn.sigmoid(jnp.dot(xn, wgate_ref[...], preferred_element_type=F32))
    pp = jnp.dot(p_ref[...].astype(BF16), wp_ref[...], preferred_element_type=F32)
    hn = x + pp * gate
    if final:
        hn = _rms(hn, fn_ref[...])
    o_ref[...] = hn


def ple_residual(h, g, p, wp, wgate, final_g, final):
    n, d = h.shape
    pd = p.shape[1]
    tm = min(_row_tile(n), 512)
    return pl.pallas_call(
        functools.partial(_ple_kernel, final=final),
        out_shape=jax.ShapeDtypeStruct((n, d), F32),
        grid=(n // tm,),
        in_specs=[pl.BlockSpec((tm, d), lambda i: (i, 0)),
                  pl.BlockSpec((1, d), lambda i: (0, 0)),
                  pl.BlockSpec((tm, pd), lambda i: (i, 0)),
                  pl.BlockSpec((pd, d), lambda i: (0, 0)),
                  pl.BlockSpec((d, d), lambda i: (0, 0)),
                  pl.BlockSpec((1, d), lambda i: (0, 0))],
        out_specs=pl.BlockSpec((tm, d), lambda i: (i, 0)),
        compiler_params=_params("parallel"),
        name="ple_residual",
    )(h, g.reshape(1, d), p, wp, wgate, final_g.reshape(1, d))


def _softplus(x):
    return jnp.maximum(x, 0.0) + jnp.log1p(jnp.exp(-jnp.abs(x)))


def _split2(x):
    hi = x.astype(BF16)
    return hi, (x - hi.astype(F32)).astype(BF16)


def _split3(x):
    hi = x.astype(BF16)
    r1 = x - hi.astype(F32)
    mid = r1.astype(BF16)
    return hi, mid, (r1 - mid.astype(F32)).astype(BF16)


def _dot3(a, b, dims=NN):
    ah, al = _split2(a)
    bh, bl = _split2(b)
    d = lambda x, y: lax.dot_general(x, y, dims, preferred_element_type=F32)
    return d(ah, bh) + (d(ah, bl) + d(al, bh))


def _dot_mask(a, m, mask_left=False):
    d = lambda x: lax.dot_general(m, x, NN, preferred_element_type=F32) if mask_left else \
        lax.dot_general(x, m, NN, preferred_element_type=F32)
    hi, mid, lo = _split3(a)
    return d(hi) + (d(mid) + d(lo))


def _rwkv_kernel(zr_ref, zk_ref, zv_ref, zl_ref, sr_ref, sk_ref, sv_ref, sl_ref,
                 mur_ref, muk_ref, muv_ref, mul_ref, w0_ref, wwa_ref, a0_ref, g2_ref,
                 kkw_ref, kaw_ref, rkw_ref, lnw_ref, lnb_ref, s0_ref,
                 hm_ref, mst_ref, min_ref, tri_ref, bd_ref, lvl_ref, eye_ref,
                 out_ref, s_out_ref, lr_ref, lk_ref, lv_ref, ll_ref, sbd_ref, *, G, C):
    L = G * HEAD_DIM
    R = G * C
    NG = RWKV_HEADS // G
    c = pl.program_id(1)
    wide = R % LANE == 0

    @pl.when(c == 0)
    def _():
        lr_ref[...] = sr_ref[0]
        lk_ref[...] = sk_ref[0]
        lv_ref[...] = sv_ref[0]
        ll_ref[...] = sl_ref[0]
        bdm = bd_ref[:L, :L].astype(F32)
        for gi in range(NG):
            sc = s0_ref[0, :, gi * L:(gi + 1) * L]
            sbd_ref[gi] = jnp.concatenate([sc] * G, axis=0) * bdm

    def shift_mix(x_ref, last_ref, mu_ref):
        x = x_ref[...]
        row = lax.broadcasted_iota(jnp.int32, x.shape, 0)
        prev = jnp.where(row == 0, last_ref[...], pltpu.roll(x, 1, axis=0))
        last_ref[...] = x[C - 1:C, :]
        return x + mu_ref[...] * (prev - x)

    r = shift_mix(zr_ref, lr_ref, mur_ref)
    k = shift_mix(zk_ref, lk_ref, muk_ref)
    v = shift_mix(zv_ref, lv_ref, muv_ref)
    lo = shift_mix(zl_ref, ll_ref, mul_ref)

    wa_in = lo[:, :W_LORA + A_LORA]
    lane = lax.broadcasted_iota(jnp.int32, wa_in.shape, 1)
    wa_in = jnp.where(lane < W_LORA, jnp.tanh(wa_in), wa_in)
    wa = jnp.dot(wa_in, wwa_ref[...], preferred_element_type=F32)
    logw = -_softplus(-(w0_ref[...] + wa[:, :RWKV_WIDTH])) - 0.5
    ld = -jnp.exp(logw)
    a = jax.nn.sigmoid(a0_ref[...] + wa[:, RWKV_WIDTH:])
    g = jnp.dot(jax.nn.sigmoid(lo[:, W_LORA + A_LORA:]), g2_ref[...], preferred_element_type=F32)

    kkv = k * kkw_ref[...]
    k2 = k * (1.0 + (a - 1.0) * kaw_ref[...])
    sums = _dot_mask(jnp.concatenate([kkv * kkv, r * k2 * rkw_ref[...]], axis=0), bd_ref[...])
    kkn = kkv / jnp.maximum(jnp.sqrt(sums[:C]), 1e-12)
    bonus = sums[C:] * v
    an = -kkn
    bn = kkn * a

    cs = _dot_mask(ld, tri_ref[...], mask_left=True)
    e_neg = jnp.exp(-cs)
    e_pos = jnp.exp(cs)
    at = an * jnp.exp(cs - ld)
    rt = r * e_pos
    bt = bn * e_neg
    kt = k2 * e_neg
    pc = e_pos[C - 1:C, :]

    hm = hm_ref[...]
    strict = mst_ref[...] > 0.5
    incl = min_ref[...] > 0.5
    groups = range(NG)
    sls = [slice(gi * L, (gi + 1) * L) for gi in groups]
    stack = lambda x, sl: jnp.concatenate([x[:, sl]] * G, axis=0) * hm
    a_s = [stack(at, sl) for sl in sls]
    r_s = [stack(rt, sl) for sl in sls]
    b_s = [stack(bt, sl) for sl in sls]
    k_s = [stack(kt, sl) for sl in sls]
    v_s = [stack(v, sl) for sl in sls]
    lhs = [jnp.concatenate([a_s[i], r_s[i]], axis=0) for i in groups]
    bk = [jnp.concatenate([b_s[i], k_s[i]], axis=0) for i in groups]
    if wide:
        m = [_dot3(lhs[i], bk[i], NT) for i in groups]
        m_b = [x[:, :R] for x in m]
        m_k = [x[:, R:] for x in m]
    else:
        m_b = [_dot3(lhs[i], b_s[i], NT) for i in groups]
        m_k = [_dot3(lhs[i], k_s[i], NT) for i in groups]
    a_ak = [jnp.where(strict, x[:R], 0.0) for x in m_k]
    a_rb = [jnp.where(incl, x[R:], 0.0) for x in m_b]
    a_rk = [jnp.where(incl, x[R:], 0.0) for x in m_k]

    sb = [sbd_ref[i] for i in groups]
    w0s = [_dot3(lhs[i], sb[i], NT) for i in groups]
    rhs = [w0s[i][:R] + _dot3(a_ak[i], v_s[i]) for i in groups]

    band = lambda i, lv: jnp.where(lvl_ref[lv] > 0.5, m_b[i][:R], 0.0)
    d = [band(i, 0) for i in groups]
    x = [eye_ref[...] + d[i] for i in groups]
    p2 = [_dot3(d[i], d[i]) for i in groups]
    if wide:
        t = [_dot3(p2[i], jnp.concatenate([x[i], p2[i]], axis=1)) for i in groups]
        x = [x[i] + t[i][:, :R] for i in groups]
        p4 = [t[i][:, R:] for i in groups]
    else:
        x, p4 = [x[i] + _dot3(p2[i], x[i]) for i in groups], [_dot3(p2[i], p2[i]) for i in groups]
    tinv = [x[i] + _dot3(p4[i], x[i]) for i in groups]
    n_lv = lvl_ref.shape[0]
    for lv in range(1, n_lv - 1):
        nt = [_dot3(band(i, lv), tinv[i]) for i in groups]
        tinv = [tinv[i] + _dot3(tinv[i], nt[i]) for i in groups]
    if n_lv == 1:
        u = [_dot3(tinv[i], rhs[i]) for i in groups]
    else:
        if wide:
            vb = [_dot3(tinv[i], jnp.concatenate([rhs[i], band(i, n_lv - 1)], axis=1)) for i in groups]
            v0 = [y[:, :L] for y in vb]
            bl = [y[:, L:] for y in vb]
        else:
            v0 = [_dot3(tinv[i], rhs[i]) for i in groups]
            bl = [_dot3(tinv[i], band(i, n_lv - 1)) for i in groups]
        u = [v0[i] + _dot3(bl[i], v0[i]) for i in groups]
    if wide:
        y = [w0s[i][R:] + _dot3(jnp.concatenate([a_rb[i], a_rk[i]], axis=1),
                                jnp.concatenate([u[i], v_s[i]], axis=0)) for i in groups]
    else:
        y = [w0s[i][R:] + _dot3(a_rb[i], u[i]) + _dot3(a_rk[i], v_s[i]) for i in groups]

    ys = []
    for i in groups:
        mu_y = jnp.sum(y[i], axis=-1, keepdims=True) * (1.0 / HEAD_DIM)
        yc = (y[i] - mu_y) * hm
        yn = yc * lax.rsqrt(jnp.sum(yc * yc, axis=-1, keepdims=True) * (1.0 / HEAD_DIM) + RWKV_LN_EPS)
        y_cl = yn[:C]
        for h in range(1, G):
            y_cl = y_cl + yn[h * C:(h + 1) * C]
        ys.append(y_cl)

    uv = [jnp.concatenate([u[i], v_s[i]], axis=0) for i in groups]
    s_new = [(sb[i] + _dot3(uv[i], bk[i], TN)) * pc[:, sls[i]] for i in groups]
    for i in groups:
        sbd_ref[i] = s_new[i]
        s_cat = s_new[i][:HEAD_DIM]
        for h in range(1, G):
            s_cat = s_cat + s_new[i][h * HEAD_DIM:(h + 1) * HEAD_DIM]
        s_out_ref[0, :, sls[i]] = s_cat
    y_all = ys[0] if NG == 1 else jnp.concatenate(ys, axis=1)
    out_ref[...] = (y_all * lnw_ref[...] + lnb_ref[...] + bonus) * g


def _rwkv_tables(G, C):
    L, R = G * HEAD_DIM, G * C
    i = np.arange(R)[:, None]
    j = np.arange(R)[None, :]
    l = np.arange(RWKV_WIDTH)
    same = (i // C) == (j // C)
    hm = ((np.arange(R)[:, None] // C) == (np.arange(L)[None, :] // HEAD_DIM))
    tri = np.arange(C)[:, None] >= np.arange(C)[None, :]
    bd = (l[:, None] // HEAD_DIM) == (l[None, :] // HEAD_DIM)
    f = lambda m, dt=np.float32: jnp.asarray(m.astype(np.float32), dtype=dt)
    assert C % RWKV_SOLVE_BLOCK == 0 and (C // RWKV_SOLVE_BLOCK) & (C // RWKV_SOLVE_BLOCK - 1) == 0
    levels, b = [], RWKV_SOLVE_BLOCK
    prev = np.zeros((R, R), bool)
    while b <= C:
        cur = ((i // b) == (j // b)) & (i > j)
        levels.append(cur & ~prev)
        prev, b = cur, 2 * b
    return [f(hm), f(same & (i > j)), f(same & (i >= j)), f(tri, BF16), f(bd, BF16),
            f(np.stack(levels)), f(np.eye(R, dtype=bool))]


def rwkv_mix(z, shift0, s0_cat, prm, B, T, G, C):
    L = G * HEAD_DIM
    NG = RWKV_HEADS // G
    NC = T // C
    W = RWKV_WIDTH
    lw = W_LORA + A_LORA + G_LORA
    lb = (3 * W) // lw
    row = lambda b, c: b * NC + c
    zspec = lambda off: pl.BlockSpec((C, W), lambda b, c: (row(b, c), off))
    sspec = lambda off: pl.BlockSpec((1, 1, W), lambda b, c: (b, 0, off))
    vspec = lambda off: pl.BlockSpec((1, W), lambda b, c: (0, off))
    const = lambda a: pl.BlockSpec(a.shape, lambda b, c: (0,) * a.ndim)
    tables = _rwkv_tables(G, C)
    mu = prm["mu"].reshape(1, RWKV_PROJ)
    sh = shift0.reshape(B, 1, RWKV_PROJ)
    in_specs = [zspec(0), zspec(1), zspec(2),
                pl.BlockSpec((C, lw), lambda b, c: (row(b, c), lb)),
                sspec(0), sspec(1), sspec(2),
                pl.BlockSpec((1, 1, lw), lambda b, c: (b, 0, lb)),
                vspec(0), vspec(1), vspec(2),
                pl.BlockSpec((1, lw), lambda b, c: (0, lb)),
                vspec(0),
                const(prm["wwa"]),
                vspec(0),
                const(prm["g2"]),
                vspec(0), vspec(0), vspec(0), vspec(0), vspec(0),
                pl.BlockSpec((1, HEAD_DIM, W), lambda b, c: (b, 0, 0))]
    in_specs += [const(t) for t in tables]
    out, s_t = pl.pallas_call(
        functools.partial(_rwkv_kernel, G=G, C=C),
        out_shape=(jax.ShapeDtypeStruct((B * T, W), F32),
                   jax.ShapeDtypeStruct((B, HEAD_DIM, W), F32)),
        grid=(B, NC),
        in_specs=in_specs,
        out_specs=(pl.BlockSpec((C, W), lambda b, c: (row(b, c), 0)),
                   pl.BlockSpec((1, HEAD_DIM, W), lambda b, c: (b, 0, 0))),
        scratch_shapes=[pltpu.VMEM((1, W), F32), pltpu.VMEM((1, W), F32),
                        pltpu.VMEM((1, W), F32), pltpu.VMEM((1, lw), F32),
                        pltpu.VMEM((NG, L, L), F32)],
        compiler_params=_params("parallel", "arbitrary"),
        name="rwkv_mix",
    )(z, z, z, z, sh, sh, sh, sh, mu, mu, mu, mu,
      prm["w0"], prm["wwa"], prm["a0"], prm["g2"], prm["k_k"], prm["k_a"], prm["r_k"],
      prm["ln_w"], prm["ln_b"], s0_cat, *tables)
    return out, s_t


def _pack_wwa(w2, a2):
    top = jnp.concatenate([w2, jnp.zeros_like(w2)], axis=1)
    bot = jnp.concatenate([jnp.zeros_like(a2), a2], axis=1)
    return jnp.concatenate([top, bot], axis=0)


def _top_blocks(gate, n_iota, n_valid, n_blocks, axis=-1):
    neg = -jnp.inf
    valid = n_iota < n_valid
    g = jnp.where(valid, gate, neg)
    sel = jnp.zeros(gate.shape, F32)
    for _ in range(MOBA_TOPK):
        m = jnp.max(g, axis=axis, keepdims=True)
        idx = jnp.min(jnp.where(g == m, n_iota, n_blocks), axis=axis, keepdims=True)
        pick = n_iota == idx
        sel = jnp.where(jnp.logical_and(pick, valid), 1.0, sel)
        g = jnp.where(pick, neg, g)
    return sel


def _top_blocks_by_rank(gate):
    n_iota = lax.broadcasted_iota(jnp.int32, gate.shape, 1)
    rank = jnp.zeros(gate.shape, F32)
    for n in range(gate.shape[1]):
        c = gate[:, n:n + 1]
        beats = jnp.logical_or(c > gate, jnp.logical_and(c == gate, n < n_iota))
        rank = rank + jnp.where(beats, 1.0, 0.0)
    return jnp.where(rank < MOBA_TOPK, 1.0, 0.0)


def _moba_prompt_kernel(q_ref, k_ref, v_ref, o_ref, kbf_ref, vt_ref, kmean_ref, *, T):
    NB = T // MOBA_BLOCK
    QB = MOBA_BLOCK
    KG = MOBA_KEY_GROUP
    BPG = KG // MOBA_BLOCK
    NH = LANE // HEAD_DIM
    blk = pl.program_id(2)
    scale = HEAD_DIM ** -0.5 * LOG2E

    @pl.when(blk == 0)
    def _():
        kf = k_ref[...]
        kbf_ref[...] = kf.astype(BF16)
        kmean_ref[...] = jnp.sum(kf.reshape(NB, MOBA_BLOCK, LANE), axis=1) * (1.0 / MOBA_BLOCK)
        for g in range(T // KG):
            vt_ref[g] = v_ref[g * KG:(g + 1) * KG, :].T.astype(BF16)

    q = q_ref[...]
    lane = lax.broadcasted_iota(jnp.int32, (QB, LANE), 1)
    n_iota = lax.broadcasted_iota(jnp.int32, (NB, QB), 0)
    kmean = kmean_ref[...]
    qbs, sels = [], []
    for hh in range(NH):
        qm = jnp.where((lane // HEAD_DIM) == hh, q, 0.0)
        gate = lax.dot_general(kmean, qm, NT, precision=HI, preferred_element_type=F32)
        qbs.append((qm * scale).astype(BF16))
        sels.append(_top_blocks(gate, n_iota, blk, NB, axis=0))

    kidx = lax.broadcasted_iota(jnp.int32, (MOBA_BLOCK, QB), 0)
    qidx = lax.broadcasted_iota(jnp.int32, (MOBA_BLOCK, QB), 1)
    n_groups = (blk * QB) // KG + 1

    def step(gi, carry, own_group):
        st = pl.multiple_of(gi * KG, KG)
        kg = kbf_ref[pl.ds(st, KG), :]
        vtg = vt_ref[gi]
        heads = range(NH)
        s = [lax.dot_general(kg, qbs[hh], NT, preferred_element_type=F32) for hh in heads]

        def mask(hh):
            parts = []
            for j in range(BPG):
                n = gi * BPG + j
                ok = jnp.max(jnp.where(n_iota == n, sels[hh], 0.0), axis=0, keepdims=True) > 0.5
                if own_group:
                    ok = jnp.logical_or(ok, jnp.logical_and(n == blk, kidx <= qidx))
                parts.append(jnp.where(ok, s[hh][j * MOBA_BLOCK:(j + 1) * MOBA_BLOCK], -jnp.inf))
            return jnp.concatenate(parts, axis=0)

        s = [mask(hh) for hh in heads]
        m_new = [jnp.maximum(carry[hh][0], jnp.max(s[hh], axis=0, keepdims=True)) for hh in heads]
        alpha = [jnp.exp2(carry[hh][0] - m_new[hh]) for hh in heads]
        p = [jnp.exp2(s[hh] - m_new[hh]) for hh in heads]
        l = [alpha[hh] * carry[hh][1] + jnp.sum(p[hh], axis=0, keepdims=True) for hh in heads]
        pv = [jnp.dot(vtg, p[hh].astype(BF16), preferred_element_type=F32) for hh in heads]
        return tuple((m_new[hh], l[hh], alpha[hh] * carry[hh][2] + pv[hh]) for hh in heads)

    init = tuple((jnp.full((1, QB), -jnp.inf, F32), jnp.zeros((1, QB), F32), jnp.zeros((LANE, QB), F32))
                 for _ in range(NH))
    first = step(n_groups - 1, init, True)
    res = lax.fori_loop(1, n_groups, lambda i, carry: step(n_groups - 1 - i, carry, False), first)
    rowi = lax.broadcasted_iota(jnp.int32, (LANE, QB), 0)
    out_t = jnp.zeros((LANE, QB), F32)
    for hh in range(NH):
        _, l, acc = res[hh]
        out_t = jnp.where((rowi // HEAD_DIM) == hh, acc / l, out_t)
    o_ref[...] = out_t.T


def moba_prompt(z, B, T):
    assert T % MOBA_KEY_GROUP == 0 and MOBA_BLOCK % MOBA_QCHUNK == 0
    NB = T // MOBA_BLOCK
    NKG = T // MOBA_KEY_GROUP
    P = MOBA_WIDTH // LANE
    q0 = RWKV_PROJ // LANE
    k0 = (RWKV_PROJ + MOBA_WIDTH) // LANE
    v0 = (RWKV_PROJ + 2 * MOBA_WIDTH) // LANE
    return pl.pallas_call(
        functools.partial(_moba_prompt_kernel, T=T),
        out_shape=jax.ShapeDtypeStruct((B * T, MOBA_WIDTH), F32),
        grid=(B, P, NB),
        in_specs=[pl.BlockSpec((MOBA_BLOCK, LANE), lambda b, p, c: (b * NB + c, q0 + p)),
                  pl.BlockSpec((T, LANE), lambda b, p, c: (b, k0 + p)),
                  pl.BlockSpec((T, LANE), lambda b, p, c: (b, v0 + p))],
        out_specs=pl.BlockSpec((MOBA_BLOCK, LANE), lambda b, p, c: (b * NB + c, p)),
        scratch_shapes=[pltpu.VMEM((T, LANE), BF16), pltpu.VMEM((NKG, LANE, MOBA_KEY_GROUP), BF16),
                        pltpu.VMEM((NB, LANE), F32)],
        compiler_params=_params("parallel", "parallel", "arbitrary"),
        name="moba_prompt",
    )(z, z, z)


def _moba_sample_kernel(pt_ref, q_ref, kn_ref, vn_ref, *refs, n_pages, tq):
    kp = refs[:n_pages]
    vp = refs[n_pages:2 * n_pages]
    hm_ref = refs[2 * n_pages]
    o_ref = refs[2 * n_pages + 1]
    H = MOBA_HEADS
    R = H * tq
    page = kp[0].shape[3]
    ppb = MOBA_BLOCK // page
    n_past = n_pages // ppb
    scale = HEAD_DIM ** -0.5 * LOG2E
    hm = hm_ref[...]
    qs = jnp.concatenate([q_ref[...]] * H, axis=0) * (hm * scale)
    qh, ql = _split2(qs)

    n_iota = lax.broadcasted_iota(jnp.int32, (R, n_past), 1)
    pages = range(n_pages)
    ks = [_split2(kp[j][0].reshape(MOBA_WIDTH, page)) for j in pages]
    s_hi = [jnp.dot(qh, ks[j][0], preferred_element_type=F32) for j in pages]
    s_x1 = [jnp.dot(qh, ks[j][1], preferred_element_type=F32) for j in pages]
    s_x2 = [jnp.dot(ql, ks[j][0], preferred_element_type=F32) for j in pages]
    fine = [s_hi[j] + (s_x1[j] + s_x2[j]) for j in pages]
    gate = jnp.zeros((R, n_past), F32)
    for n in range(n_past):
        blk_el = fine[n * ppb]
        for j in range(1, ppb):
            blk_el = blk_el + fine[n * ppb + j]
        gate = jnp.where(n_iota == n, jnp.sum(blk_el, axis=-1, keepdims=True), gate)
    sel = _top_blocks_by_rank(gate)

    scores = [jnp.where(sel[:, j // ppb:j // ppb + 1] > 0.5, s_hi[j], -jnp.inf) for j in range(n_pages)]
    s_own = lax.dot_general(qh, kn_ref[...].astype(BF16), NT, preferred_element_type=F32)
    qi = lax.broadcasted_iota(jnp.int32, (R, tq), 0) % tq
    ki = lax.broadcasted_iota(jnp.int32, (R, tq), 1)
    s_own = jnp.where(ki <= qi, s_own, -jnp.inf)

    m_el = scores[0]
    for j in range(1, n_pages):
        m_el = jnp.maximum(m_el, scores[j])
    m = jnp.maximum(jnp.max(s_own, axis=-1, keepdims=True), jnp.max(m_el, axis=-1, keepdims=True))
    p_own = jnp.exp2(s_own - m)
    acc = jnp.dot(p_own.astype(BF16), vn_ref[...].astype(BF16), preferred_element_type=F32)
    p = [jnp.exp2(scores[j] - m) for j in pages]
    vt = [vp[j][0].reshape(MOBA_WIDTH, page).astype(BF16) for j in pages]
    pv = [lax.dot_general(p[j].astype(BF16), vt[j], NT, preferred_element_type=F32) for j in pages]
    l_el = p[0]
    for j in range(1, n_pages):
        l_el = l_el + p[j]
        acc = acc + pv[j]
    acc = acc + pv[0]
    l = jnp.sum(p_own, axis=-1, keepdims=True) + jnp.sum(l_el, axis=-1, keepdims=True)
    om = (acc / l) * hm
    out = om[:tq]
    for h in range(1, H):
        out = out + om[h * tq:(h + 1) * tq]
    o_ref[...] = out


def moba_sample(zq, zk, zv, pool_k, pool_v, page_table, B, T):
    n_pages = page_table.shape[1]
    n_pool, page_size = pool_k.shape[0], pool_k.shape[1]
    assert MOBA_BLOCK % page_size == 0 and (n_pages * page_size) % MOBA_BLOCK == 0
    assert T <= MOBA_BLOCK and T % MOBA_QCHUNK != 0 and n_pages * page_size // MOBA_BLOCK >= MOBA_TOPK
    H = MOBA_HEADS
    R = H * T
    pk = pool_k.transpose(0, 2, 3, 1)
    pv = pool_v.transpose(0, 2, 3, 1)
    hm = jnp.asarray(((np.arange(R)[:, None] // T) == (np.arange(MOBA_WIDTH)[None, :] // HEAD_DIM))
                     .astype(np.float32))
    rows = pl.BlockSpec((T, MOBA_WIDTH), lambda b, pt: (b, 0))
    page = lambda j: pl.BlockSpec((1, H, HEAD_DIM, page_size), lambda b, pt: (pt[b, j], 0, 0, 0))
    return pl.pallas_call(
        functools.partial(_moba_sample_kernel, n_pages=n_pages, tq=T),
        out_shape=jax.ShapeDtypeStruct((B * T, MOBA_WIDTH), F32),
        grid_spec=pltpu.PrefetchScalarGridSpec(
            num_scalar_prefetch=1,
            grid=(B,),
            in_specs=[rows, rows, rows] + [page(j) for j in range(n_pages)] * 2
            + [pl.BlockSpec(hm.shape, lambda b, pt: (0, 0))],
            out_specs=rows),
        compiler_params=_params("parallel"),
        name="moba_sample",
    )(page_table, zq, zk, zv, *([pk] * n_pages), *([pv] * n_pages), hm)


def _retention_kernel(q_ref, k_ref, v_ref, g_ref, cos_ref, sin_ref, lg_ref, gnw_ref, s0_ref,
                      o_ref, s_out_ref, *, C):
    c = pl.program_id(1)

    @pl.when(c == 0)
    def _():
        s_out_ref[...] = s0_ref[...]

    half = RET_DK // 2
    heads = range(RET_HEADS)
    cos = cos_ref[...]
    sin = sin_ref[...]

    def rot(x_ref, h):
        x1 = x_ref[:, h * RET_DK:h * RET_DK + half]
        x2 = x_ref[:, h * RET_DK + half:(h + 1) * RET_DK]
        return jnp.concatenate([x1 * cos - x2 * sin, x1 * sin + x2 * cos], axis=-1)

    ii = lax.broadcasted_iota(jnp.int32, (C, C), 0)
    jj = lax.broadcasted_iota(jnp.int32, (C, C), 1)
    diff = jnp.maximum((ii - jj).astype(F32), 0.0)
    it = lax.broadcasted_iota(jnp.int32, (C, LANE), 0).astype(F32)
    lg = [lg_ref[h] for h in heads]
    dmask = [jnp.where(ii >= jj, jnp.exp(diff * lg[h][:, :C]), 0.0) for h in heads]
    cross = [jnp.exp((it + 1.0) * lg[h])[:, :1] for h in heads]
    kdec = [jnp.exp((C - 1.0 - it) * lg[h])[:, :1] for h in heads]
    sdec = [jnp.exp(C * lg[h])[:, :1] for h in heads]

    qb = [rot(q_ref, h).astype(BF16) for h in heads]
    kr = [rot(k_ref, h) * (RET_DK ** -0.5) for h in heads]
    v = [v_ref[:, h * RET_DV:(h + 1) * RET_DV] for h in heads]
    vb = [x.astype(BF16) for x in v]
    s = [s_out_ref[0, h] for h in heads]
    att = [lax.dot_general(qb[h], kr[h].astype(BF16), NT, preferred_element_type=F32) * dmask[h]
           for h in heads]
    y_in = [jnp.dot(att[h].astype(BF16), vb[h], preferred_element_type=F32) for h in heads]
    y_x = [jnp.dot(qb[h], s[h].astype(BF16), preferred_element_type=F32) * cross[h] for h in heads]
    for h in heads:
        s_out_ref[0, h] = s[h] * sdec[h] + lax.dot_general(kr[h] * kdec[h], v[h], TN,
                                                           preferred_element_type=F32)
    for h in heads:
        y = y_in[h] + y_x[h]
        mu = jnp.mean(y, axis=-1, keepdims=True)
        yc = y - mu
        sl = slice(h * RET_DV, (h + 1) * RET_DV)
        yn = yc * lax.rsqrt(jnp.mean(yc * yc, axis=-1, keepdims=True) + GN_EPS) * gnw_ref[:, sl]
        g = g_ref[:, sl]
        o_ref[:, sl] = (g * jax.nn.sigmoid(g)) * yn


def retention_mix(z, s0, gn_w, B, T, pos0):
    C = RET_CHUNK if T % RET_CHUNK == 0 else T
    NC = T // C
    half = RET_DK // 2
    pos = (pos0 + jnp.arange(T)).astype(F32)
    inv = ROPE_BASE ** (-jnp.arange(half, dtype=F32) / half)
    ang = pos[:, None] * inv[None, :]
    log_g = jnp.log1p(-jnp.exp2(-5.0 - jnp.arange(RET_HEADS, dtype=F32)))
    lg = jnp.broadcast_to(log_g[:, None, None], (RET_HEADS, 1, LANE))
    row = lambda b, c: b * NC + c
    state = pl.BlockSpec((1, RET_HEADS, RET_DK, RET_DV), lambda b, c: (b, 0, 0, 0))
    return pl.pallas_call(
        functools.partial(_retention_kernel, C=C),
        out_shape=(jax.ShapeDtypeStruct((B * T, RET_V), F32),
                   jax.ShapeDtypeStruct((B, RET_HEADS, RET_DK, RET_DV), F32)),
        grid=(B, NC),
        in_specs=[pl.BlockSpec((C, RET_QK), lambda b, c: (row(b, c), 0)),
                  pl.BlockSpec((C, RET_QK), lambda b, c: (row(b, c), 1)),
                  pl.BlockSpec((C, RET_V), lambda b, c: (row(b, c), 2 * RET_QK // RET_V)),
                  pl.BlockSpec((C, RET_V), lambda b, c: (row(b, c), 2 * RET_QK // RET_V + 1)),
                  pl.BlockSpec((C, half), lambda b, c: (c, 0)),
                  pl.BlockSpec((C, half), lambda b, c: (c, 0)),
                  pl.BlockSpec((RET_HEADS, 1, LANE), lambda b, c: (0, 0, 0)),
                  pl.BlockSpec((1, RET_V), lambda b, c: (0, 0)),
                  state],
        out_specs=(pl.BlockSpec((C, RET_V), lambda b, c: (row(b, c), 0)), state),
        compiler_params=_params("parallel", "arbitrary"),
        name="retention_mix",
    )(z, z, z, z, jnp.cos(ang), jnp.sin(ang), lg, gn_w.reshape(1, RET_V), s0)


def _run_trunk(x, p, B, T, pos0, moba_fn, rwkv_s0, rwkv_shift0, ret_s0, G, C, W):
    h = x
    z = norm_matmul(h, W["norm_mix"][0], W["a_w_in"][0], tn=A_IN // 2)
    s0_cat = rwkv_s0.transpose(0, 2, 1, 3).reshape(B, HEAD_DIM, RWKV_WIDTH)
    rwkv_out, s_cat = rwkv_mix(z, rwkv_shift0, s0_cat, W["rwkv"], B, T, G, C)
    rwkv_state = s_cat.reshape(B, HEAD_DIM, RWKV_HEADS, HEAD_DIM).transpose(0, 2, 1, 3)
    zk = z[:, RWKV_PROJ + MOBA_WIDTH:RWKV_PROJ + 2 * MOBA_WIDTH]
    zv = z[:, RWKV_PROJ + 2 * MOBA_WIDTH:]
    moba_out = moba_fn(z, zk, zv)
    shift_t = z.reshape(B, T, A_IN)[:, T - 1, :RWKV_PROJ]
    wo = W["a_w_out"][0]
    h = proj_residual(h, [rwkv_out, moba_out], [wo[:RWKV_WIDTH], wo[RWKV_WIDTH:]])
    h = ffn_residual(h, W["norm_ffn"][0], W["ffn_w_gu"][0], W["ffn_w_down"][0])
    h = ple_residual(h, W["ple_norm"][0], p[0], W["ple_proj"][0], W["ple_gate"][0],
                     W["final_norm"], final=False)
    z2 = norm_matmul(h, W["norm_mix"][1], W["c_w_in"][0], tn=C_IN // 3)
    gated, ret_state = retention_mix(z2, ret_s0, W["ret_gn_w"][0], B, T, pos0)
    h = proj_residual(h, [gated], [W["c_w_out"][0]])
    h = ffn_residual(h, W["norm_ffn"][1], W["ffn_w_gu"][1], W["ffn_w_down"][1])
    y = ple_residual(h, W["ple_norm"][1], p[1], W["ple_proj"][1], W["ple_gate"][1],
                     W["final_norm"], final=True)
    k_rows = zk.reshape(1, B, T, MOBA_HEADS, HEAD_DIM)
    v_rows = zv.reshape(1, B, T, MOBA_HEADS, HEAD_DIM)
    return (y.reshape(B, T, D_MODEL), k_rows, v_rows, rwkv_state[None],
            shift_t[None], ret_state[None])


def kernel(x_prompt, x_sample, cache_moba_k, cache_moba_v, state_rwkv, state_rwkv_shift, state_ret, page_table, p_prompt, p_sample, norm_mix, norm_ffn, ffn_w_gate, ffn_w_up, ffn_w_down, ple_norm, ple_gate, ple_proj, a_w_in, rwkv_mu, rwkv_w0, rwkv_w2, rwkv_a0, rwkv_a2, rwkv_g2, rwkv_k_k, rwkv_k_a, rwkv_r_k, rwkv_ln_w, rwkv_ln_b, a_w_out, c_w_in, ret_gn_w, c_w_out, final_norm):
    assert norm_mix.shape[0] == DEPTH == 2
    B, T, _ = x_prompt.shape
    Bd, Td, _ = x_sample.shape
    bf = lambda w: w.astype(BF16)
    row = lambda a: a.reshape(1, RWKV_WIDTH)
    rwkv = dict(mu=rwkv_mu[0], w0=row(rwkv_w0[0]), a0=row(rwkv_a0[0]), g2=rwkv_g2[0],
                k_k=row(rwkv_k_k[0]), k_a=row(rwkv_k_a[0]), r_k=row(rwkv_r_k[0]),
                ln_w=row(rwkv_ln_w[0]), ln_b=row(rwkv_ln_b[0]),
                wwa=_pack_wwa(rwkv_w2[0], rwkv_a2[0]))
    ffn_w_gu = [_pack_gate_up(bf(ffn_w_gate[i]), bf(ffn_w_up[i])) for i in range(DEPTH)]
    W = dict(norm_mix=norm_mix, norm_ffn=norm_ffn, ffn_w_gu=ffn_w_gu, ffn_w_down=bf(ffn_w_down), ple_norm=ple_norm, ple_gate=bf(ple_gate), ple_proj=bf(ple_proj),
             a_w_in=bf(a_w_in), a_w_out=bf(a_w_out), c_w_in=bf(c_w_in), c_w_out=bf(c_w_out),
             ret_gn_w=ret_gn_w, final_norm=final_norm, rwkv=rwkv)

    yp, kp, vp, rsp, shp, rtp = _run_trunk(
        x_prompt.reshape(B * T, D_MODEL), p_prompt.reshape(DEPTH, B * T, PLE_DIM), B, T, 0,
        lambda z, zk, zv: moba_prompt(z, B, T),
        jnp.zeros((B, RWKV_HEADS, HEAD_DIM, HEAD_DIM), F32), jnp.zeros((B, RWKV_PROJ), F32),
        jnp.zeros((B, RET_HEADS, RET_DK, RET_DV), F32), RWKV_GROUP, min(RWKV_CHUNK, T), W)

    past_len = page_table.shape[1] * cache_moba_k.shape[2]
    q_lo = RWKV_PROJ
    ys, ks, vs, rss, shs, rts = _run_trunk(
        x_sample.reshape(Bd * Td, D_MODEL), p_sample.reshape(DEPTH, Bd * Td, PLE_DIM), Bd, Td, past_len,
        lambda z, zk, zv: moba_sample(z[:, q_lo:q_lo + MOBA_WIDTH], zk, zv, cache_moba_k[0],
                                      cache_moba_v[0], page_table, Bd, Td),
        state_rwkv[0], state_rwkv_shift[0], state_ret[0], RWKV_GROUP, min(RWKV_CHUNK, Td), W)

    return (yp, ys, kp, vp, ks, vs, rsp, rss, shp, shs, rtp, rts)
```

```python
import functools
import math

import numpy as np
import jax
import jax.numpy as jnp
from jax import lax
from jax.experimental import pallas as pl
from jax.experimental.pallas import tpu as pltpu

F32 = jnp.float32
BF16 = jnp.bfloat16
HI = lax.Precision.HIGHEST

D_MODEL = 1024
DEPTH = 2
PLE_DIM = 256
HEAD_DIM = 64
RWKV_HEADS = 8
RWKV_WIDTH = RWKV_HEADS * HEAD_DIM
W_LORA = 64
A_LORA = 64
G_LORA = 128
RWKV_PROJ = 3 * RWKV_WIDTH + W_LORA + A_LORA + G_LORA
RWKV_LN_EPS = 64e-5
RWKV_CHUNK = 64
RWKV_GROUP = 2
RWKV_SOLVE_BLOCK = 8
MOBA_HEADS = 8
MOBA_WIDTH = MOBA_HEADS * HEAD_DIM
MOBA_BLOCK = 256
MOBA_TOPK = 3
MOBA_QCHUNK = 128
MOBA_KEY_GROUP = 4 * MOBA_BLOCK
A_IN = RWKV_PROJ + 3 * MOBA_WIDTH
RET_HEADS = 4
RET_DK = 256
RET_DV = 512
RET_QK = RET_HEADS * RET_DK
RET_V = RET_HEADS * RET_DV
C_IN = 2 * RET_QK + 2 * RET_V
RET_CHUNK = 128
ROPE_BASE = 10000.0
D_FF = 2816
NORM_EPS = 1e-6
GN_EPS = 1e-5
LOG2E = math.log2(math.e)

LANE = 128
VMEM_LIMIT = 56 * 1024 * 1024

NN = (((1,), (0,)), ((), ()))
NT = (((1,), (1,)), ((), ()))
TN = (((0,), (0,)), ((), ()))


def _params(*sem):
    return pltpu.CompilerParams(dimension_semantics=sem, vmem_limit_bytes=VMEM_LIMIT)


def _rms(x, g):
    return x * lax.rsqrt(jnp.mean(x * x, axis=-1, keepdims=True) + NORM_EPS) * g


def _row_tile(n):
    for t in (1024, 512, 256, 128, 64, 32, 16, 8):
        if n % t == 0:
            return t
    raise ValueError(n)


def _norm_matmul_kernel(x_ref, g_ref, w_ref, *rest, n_t):
    if n_t:
        wt_ref, o_ref = rest[:2]
        t_refs = rest[2:2 + n_t]
    else:
        o_ref = rest[0]
    xn_ref = rest[-1]

    @pl.when(pl.program_id(1) == 0)
    def _():
        xn_ref[...] = _rms(x_ref[...], g_ref[...]).astype(BF16)

    o_ref[...] = jnp.dot(xn_ref[...], w_ref[...], preferred_element_type=F32)

    if n_t:
        @pl.when(pl.program_id(1) == pl.num_programs(1) - 1)
        def _():
            yt = lax.dot_general(wt_ref[...], xn_ref[...], NT, preferred_element_type=F32)
            cw = yt.shape[0] // n_t
            for k, t_ref in enumerate(t_refs):
                t_ref[0] = yt[k * cw:(k + 1) * cw]


def norm_matmul(x, g, w, tn, transposed=None):
    n, d = x.shape
    f = w.shape[1]
    tm = _row_tile(n)
    in_specs = [pl.BlockSpec((tm, d), lambda i, j: (i, 0)),
                pl.BlockSpec((1, d), lambda i, j: (0, 0)),
                pl.BlockSpec((d, tn), lambda i, j: (0, j))]
    out_shape = [jax.ShapeDtypeStruct((n, f), F32)]
    out_specs = [pl.BlockSpec((tm, tn), lambda i, j: (i, j))]
    args = [x, g.reshape(1, d), w]
    n_t = 0
    if transposed is not None:
        wt, n_t, b, t = transposed
        cw = wt.shape[0] // n_t
        nt = t // tm
        in_specs.append(pl.BlockSpec(wt.shape, lambda i, j: (0, 0)))
        args.append(wt)
        out_shape += [jax.ShapeDtypeStruct((b, cw, t), F32)] * n_t
        out_specs += [pl.BlockSpec((1, cw, tm), lambda i, j: (i // nt, 0, i % nt))] * n_t
    res = pl.pallas_call(
        functools.partial(_norm_matmul_kernel, n_t=n_t),
        out_shape=out_shape,
        grid=(n // tm, f // tn),
        in_specs=in_specs,
        out_specs=out_specs,
        scratch_shapes=[pltpu.VMEM((tm, d), BF16)],
        compiler_params=_params("parallel", "arbitrary"),
        name="norm_matmul",
    )(*args)
    return res if n_t else res[0]


def _proj_res_kernel(*refs, n_x):
    h_ref = refs[0]
    xs = refs[1:1 + n_x]
    ws = refs[1 + n_x:1 + 2 * n_x]
    o_ref = refs[1 + 2 * n_x]
    acc = h_ref[...]
    for x_ref, w_ref in zip(xs, ws):
        acc = acc + jnp.dot(x_ref[...].astype(BF16), w_ref[...], preferred_element_type=F32)
    o_ref[...] = acc


def proj_residual(h, xs, ws):
    n, d = h.shape
    tm = min(_row_tile(n), 512)
    in_specs = [pl.BlockSpec((tm, d), lambda i: (i, 0))]
    in_specs += [pl.BlockSpec((tm, x.shape[1]), lambda i: (i, 0)) for x in xs]
    in_specs += [pl.BlockSpec(w.shape, lambda i: (0, 0)) for w in ws]
    return pl.pallas_call(
        functools.partial(_proj_res_kernel, n_x=len(xs)),
        out_shape=jax.ShapeDtypeStruct((n, d), F32),
        grid=(n // tm,),
        in_specs=in_specs,
        out_specs=pl.BlockSpec((tm, d), lambda i: (i, 0)),
        compiler_params=_params("parallel"),
        name="proj_residual",
    )(h, *xs, *ws)


FFN_TILE = 256


def _ffn_kernel(h_ref, g_ref, wg_ref, wu_ref, wd_ref, o_ref, xn_ref):
    @pl.when(pl.program_id(1) == 0)
    def _():
        x = h_ref[...]
        xn_ref[...] = _rms(x, g_ref[...]).astype(BF16)
        o_ref[...] = x

    xn = xn_ref[...]
    a = jnp.dot(xn, wg_ref[...], preferred_element_type=F32)
    b = jnp.dot(xn, wu_ref[...], preferred_element_type=F32)
    t = (a * jax.nn.sigmoid(a)) * b
    o_ref[...] += jnp.dot(t.astype(BF16), wd_ref[...], preferred_element_type=F32)


def ffn_residual(h, g, wg, wu, wd):
    n, d = h.shape
    ff = wg.shape[1]
    tf = FFN_TILE
    tm = _row_tile(n)
    return pl.pallas_call(
        _ffn_kernel,
        out_shape=jax.ShapeDtypeStruct((n, d), F32),
        grid=(n // tm, ff // tf),
        in_specs=[pl.BlockSpec((tm, d), lambda i, j: (i, 0)),
                  pl.BlockSpec((1, d), lambda i, j: (0, 0)),
                  pl.BlockSpec((d, tf), lambda i, j: (0, j)),
                  pl.BlockSpec((d, tf), lambda i, j: (0, j)),
                  pl.BlockSpec((tf, d), lambda i, j: (j, 0))],
        out_specs=pl.BlockSpec((tm, d), lambda i, j: (i, 0)),
        scratch_shapes=[pltpu.VMEM((tm, d), BF16)],
        compiler_params=_params("parallel", "arbitrary"),
        name="ffn_residual",
    )(h, g.reshape(1, d), wg, wu, wd)


def _ple_kernel(h_ref, g_ref, p_ref, wp_ref, wgate_ref, fn_ref, o_ref, *, final):
    x = h_ref[...]
    xn = _rms(x, g_ref[...]).astype(BF16)
    gate = jax.nn.sigmoid(jnp.dot(xn, wgate_ref[...], preferred_element_type=F32))
    pp = jnp.dot(p_ref[...].astype(BF16), wp_ref[...], preferred_element_type=F32)
    hn = x + pp * gate
    if final:
        hn = _rms(hn, fn_ref[...])
    o_ref[...] = hn


def ple_residual(h, g, p, wp, wgate, final_g, final):
    n, d = h.shape
    pd = p.shape[1]
    tm = min(_row_tile(n), 512)
    return pl.pallas_call(
        functools.partial(_ple_kernel, final=final),
        out_shape=jax.ShapeDtypeStruct((n, d), F32),
        grid=(n // tm,),
        in_specs=[pl.BlockSpec((tm, d), lambda i: (i, 0)),
                  pl.BlockSpec((1, d), lambda i: (0, 0)),
                  pl.BlockSpec((tm, pd), lambda i: (i, 0)),
                  pl.BlockSpec((pd, d), lambda i: (0, 0)),
                  pl.BlockSpec((d, d), lambda i: (0, 0)),
                  pl.BlockSpec((1, d), lambda i: (0, 0))],
        out_specs=pl.BlockSpec((tm, d), lambda i: (i, 0)),
        compiler_params=_params("parallel"),
        name="ple_residual",
    )(h, g.reshape(1, d), p, wp, wgate, final_g.reshape(1, d))


def _softplus(x):
    return jnp.maximum(x, 0.0) + jnp.log1p(jnp.exp(-jnp.abs(x)))


def _split2(x):
    hi = x.astype(BF16)
    return hi, (x - hi.astype(F32)).astype(BF16)


def _split3(x):
    hi = x.astype(BF16)
    r1 = x - hi.astype(F32)
    mid = r1.astype(BF16)
    return hi, mid, (r1 - mid.astype(F32)).astype(BF16)


def _dot3(a, b, dims=NN):
    ah, al = _split2(a)
    bh, bl = _split2(b)
    d = lambda x, y: lax.dot_general(x, y, dims, preferred_element_type=F32)
    return d(ah, bh) + (d(ah, bl) + d(al, bh))


def _dot_mask(a, m, mask_left=False):
    d = lambda x: lax.dot_general(m, x, NN, preferred_element_type=F32) if mask_left else \
        lax.dot_general(x, m, NN, preferred_element_type=F32)
    hi, mid, lo = _split3(a)
    return d(hi) + (d(mid) + d(lo))


def _rwkv_kernel(zr_ref, zk_ref, zv_ref, zl_ref, sr_ref, sk_ref, sv_ref, sl_ref,
                 mur_ref, muk_ref, muv_ref, mul_ref, w0_ref, wwa_ref, a0_ref, g2_ref,
                 kkw_ref, kaw_ref, rkw_ref, lnw_ref, lnb_ref, s0_ref,
                 hm_ref, mst_ref, min_ref, tri_ref, bd_ref, lvl_ref, eye_ref,
                 out_ref, s_out_ref, lr_ref, lk_ref, lv_ref, ll_ref, sbd_ref, *, G, C):
    L = G * HEAD_DIM
    R = G * C
    NG = RWKV_HEADS // G
    c = pl.program_id(1)
    wide = R % LANE == 0

    @pl.when(c == 0)
    def _():
        lr_ref[...] = sr_ref[0]
        lk_ref[...] = sk_ref[0]
        lv_ref[...] = sv_ref[0]
        ll_ref[...] = sl_ref[0]
        bdm = bd_ref[:L, :L].astype(F32)
        for gi in range(NG):
            sc = s0_ref[0, :, gi * L:(gi + 1) * L]
            sbd_ref[gi] = jnp.concatenate([sc] * G, axis=0) * bdm

    def shift_mix(x_ref, last_ref, mu_ref):
        x = x_ref[...]
        row = lax.broadcasted_iota(jnp.int32, x.shape, 0)
        prev = jnp.where(row == 0, last_ref[...], pltpu.roll(x, 1, axis=0))
        last_ref[...] = x[C - 1:C, :]
        return x + mu_ref[...] * (prev - x)

    r = shift_mix(zr_ref, lr_ref, mur_ref)
    k = shift_mix(zk_ref, lk_ref, muk_ref)
    v = shift_mix(zv_ref, lv_ref, muv_ref)
    lo = shift_mix(zl_ref, ll_ref, mul_ref)

    wa_in = lo[:, :W_LORA + A_LORA]
    lane = lax.broadcasted_iota(jnp.int32, wa_in.shape, 1)
    wa_in = jnp.where(lane < W_LORA, jnp.tanh(wa_in), wa_in)
    wa = jnp.dot(wa_in, wwa_ref[...], preferred_element_type=F32)
    logw = -_softplus(-(w0_ref[...] + wa[:, :RWKV_WIDTH])) - 0.5
    ld = -jnp.exp(logw)
    a = jax.nn.sigmoid(a0_ref[...] + wa[:, RWKV_WIDTH:])
    g = jnp.dot(jax.nn.sigmoid(lo[:, W_LORA + A_LORA:]), g2_ref[...], preferred_element_type=F32)

    kkv = k * kkw_ref[...]
    k2 = k * (1.0 + (a - 1.0) * kaw_ref[...])
    sums = _dot_mask(jnp.concatenate([kkv * kkv, r * k2 * rkw_ref[...]], axis=0), bd_ref[...])
    kkn = kkv / jnp.maximum(jnp.sqrt(sums[:C]), 1e-12)
    bonus = sums[C:] * v
    an = -kkn
    bn = kkn * a

    cs = _dot_mask(ld, tri_ref[...], mask_left=True)
    e_neg = jnp.exp(-cs)
    e_pos = jnp.exp(cs)
    at = an * jnp.exp(cs - ld)
    rt = r * e_pos
    bt = bn * e_neg
    kt = k2 * e_neg
    pc = e_pos[C - 1:C, :]

    hm = hm_ref[...]
    strict = mst_ref[...] > 0.5
    incl = min_ref[...] > 0.5
    groups = range(NG)
    sls = [slice(gi * L, (gi + 1) * L) for gi in groups]
    stack = lambda x, sl: jnp.concatenate([x[:, sl]] * G, axis=0) * hm
    a_s = [stack(at, sl) for sl in sls]
    r_s = [stack(rt, sl) for sl in sls]
    b_s = [stack(bt, sl) for sl in sls]
    k_s = [stack(kt, sl) for sl in sls]
    v_s = [stack(v, sl) for sl in sls]
    lhs = [jnp.concatenate([a_s[i], r_s[i]], axis=0) for i in groups]
    bk = [jnp.concatenate([b_s[i], k_s[i]], axis=0) for i in groups]
    if wide:
        m = [_dot3(lhs[i], bk[i], NT) for i in groups]
        m_b = [x[:, :R] for x in m]
        m_k = [x[:, R:] for x in m]
    else:
        m_b = [_dot3(lhs[i], b_s[i], NT) for i in groups]
        m_k = [_dot3(lhs[i], k_s[i], NT) for i in groups]
    a_ak = [jnp.where(strict, x[:R], 0.0) for x in m_k]
    a_rb = [jnp.where(incl, x[R:], 0.0) for x in m_b]
    a_rk = [jnp.where(incl, x[R:], 0.0) for x in m_k]

    sb = [sbd_ref[i] for i in groups]
    w0s = [_dot3(lhs[i], sb[i], NT) for i in groups]
    rhs = [w0s[i][:R] + _dot3(a_ak[i], v_s[i]) for i in groups]

    band = lambda i, lv: jnp.where(lvl_ref[lv] > 0.5, m_b[i][:R], 0.0)
    d = [band(i, 0) for i in groups]
    x = [eye_ref[...] + d[i] for i in groups]
    p2 = [_dot3(d[i], d[i]) for i in groups]
    if wide:
        t = [_dot3(p2[i], jnp.concatenate([x[i], p2[i]], axis=1)) for i in groups]
        x = [x[i] + t[i][:, :R] for i in groups]
        p4 = [t[i][:, R:] for i in groups]
    else:
        x, p4 = [x[i] + _dot3(p2[i], x[i]) for i in groups], [_dot3(p2[i], p2[i]) for i in groups]
    tinv = [x[i] + _dot3(p4[i], x[i]) for i in groups]
    n_lv = lvl_ref.shape[0]
    for lv in range(1, n_lv - 1):
        nt = [_dot3(band(i, lv), tinv[i]) for i in groups]
        tinv = [tinv[i] + _dot3(tinv[i], nt[i]) for i in groups]
    if n_lv == 1:
        u = [_dot3(tinv[i], rhs[i]) for i in groups]
    else:
        if wide:
            vb = [_dot3(tinv[i], jnp.concatenate([rhs[i], band(i, n_lv - 1)], axis=1)) for i in groups]
            v0 = [y[:, :L] for y in vb]
            bl = [y[:, L:] for y in vb]
        else:
            v0 = [_dot3(tinv[i], rhs[i]) for i in groups]
            bl = [_dot3(tinv[i], band(i, n_lv - 1)) for i in groups]
        u = [v0[i] + _dot3(bl[i], v0[i]) for i in groups]
    if wide:
        y = [w0s[i][R:] + _dot3(jnp.concatenate([a_rb[i], a_rk[i]], axis=1),
                                jnp.concatenate([u[i], v_s[i]], axis=0)) for i in groups]
    else:
        y = [w0s[i][R:] + _dot3(a_rb[i], u[i]) + _dot3(a_rk[i], v_s[i]) for i in groups]

    ys = []
    for i in groups:
        mu_y = jnp.sum(y[i], axis=-1, keepdims=True) * (1.0 / HEAD_DIM)
        yc = (y[i] - mu_y) * hm
        yn = yc * lax.rsqrt(jnp.sum(yc * yc, axis=-1, keepdims=True) * (1.0 / HEAD_DIM) + RWKV_LN_EPS)
        y_cl = yn[:C]
        for h in range(1, G):
            y_cl = y_cl + yn[h * C:(h + 1) * C]
        ys.append(y_cl)

    uv = [jnp.concatenate([u[i], v_s[i]], axis=0) for i in groups]
    s_new = [(sb[i] + _dot3(uv[i], bk[i], TN)) * pc[:, sls[i]] for i in groups]
    for i in groups:
        sbd_ref[i] = s_new[i]
        s_cat = s_new[i][:HEAD_DIM]
        for h in range(1, G):
            s_cat = s_cat + s_new[i][h * HEAD_DIM:(h + 1) * HEAD_DIM]
        s_out_ref[0, :, sls[i]] = s_cat
    y_all = ys[0] if NG == 1 else jnp.concatenate(ys, axis=1)
    out_ref[...] = (y_all * lnw_ref[...] + lnb_ref[...] + bonus) * g


def _rwkv_tables(G, C):
    L, R = G * HEAD_DIM, G * C
    i = np.arange(R)[:, None]
    j = np.arange(R)[None, :]
    l = np.arange(RWKV_WIDTH)
    same = (i // C) == (j // C)
    hm = ((np.arange(R)[:, None] // C) == (np.arange(L)[None, :] // HEAD_DIM))
    tri = np.arange(C)[:, None] >= np.arange(C)[None, :]
    bd = (l[:, None] // HEAD_DIM) == (l[None, :] // HEAD_DIM)
    f = lambda m, dt=np.float32: jnp.asarray(m.astype(np.float32), dtype=dt)
    assert C % RWKV_SOLVE_BLOCK == 0 and (C // RWKV_SOLVE_BLOCK) & (C // RWKV_SOLVE_BLOCK - 1) == 0
    levels, b = [], RWKV_SOLVE_BLOCK
    prev = np.zeros((R, R), bool)
    while b <= C:
        cur = ((i // b) == (j // b)) & (i > j)
        levels.append(cur & ~prev)
        prev, b = cur, 2 * b
    return [f(hm), f(same & (i > j)), f(same & (i >= j)), f(tri, BF16), f(bd, BF16),
            f(np.stack(levels)), f(np.eye(R, dtype=bool))]


def rwkv_mix(z, shift0, s0_cat, prm, B, T, G, C):
    L = G * HEAD_DIM
    NG = RWKV_HEADS // G
    NC = T // C
    W = RWKV_WIDTH
    lw = W_LORA + A_LORA + G_LORA
    lb = (3 * W) // lw
    row = lambda b, c: b * NC + c
    zspec = lambda off: pl.BlockSpec((C, W), lambda b, c: (row(b, c), off))
    sspec = lambda off: pl.BlockSpec((1, 1, W), lambda b, c: (b, 0, off))
    vspec = lambda off: pl.BlockSpec((1, W), lambda b, c: (0, off))
    const = lambda a: pl.BlockSpec(a.shape, lambda b, c: (0,) * a.ndim)
    tables = _rwkv_tables(G, C)
    mu = prm["mu"].reshape(1, RWKV_PROJ)
    sh = shift0.reshape(B, 1, RWKV_PROJ)
    in_specs = [zspec(0), zspec(1), zspec(2),
                pl.BlockSpec((C, lw), lambda b, c: (row(b, c), lb)),
                sspec(0), sspec(1), sspec(2),
                pl.BlockSpec((1, 1, lw), lambda b, c: (b, 0, lb)),
                vspec(0), vspec(1), vspec(2),
                pl.BlockSpec((1, lw), lambda b, c: (0, lb)),
                vspec(0),
                const(prm["wwa"]),
                vspec(0),
                const(prm["g2"]),
                vspec(0), vspec(0), vspec(0), vspec(0), vspec(0),
                pl.BlockSpec((1, HEAD_DIM, W), lambda b, c: (b, 0, 0))]
    in_specs += [const(t) for t in tables]
    out, s_t = pl.pallas_call(
        functools.partial(_rwkv_kernel, G=G, C=C),
        out_shape=(jax.ShapeDtypeStruct((B * T, W), F32),
                   jax.ShapeDtypeStruct((B, HEAD_DIM, W), F32)),
        grid=(B, NC),
        in_specs=in_specs,
        out_specs=(pl.BlockSpec((C, W), lambda b, c: (row(b, c), 0)),
                   pl.BlockSpec((1, HEAD_DIM, W), lambda b, c: (b, 0, 0))),
        scratch_shapes=[pltpu.VMEM((1, W), F32), pltpu.VMEM((1, W), F32),
                        pltpu.VMEM((1, W), F32), pltpu.VMEM((1, lw), F32),
                        pltpu.VMEM((NG, L, L), F32)],
        compiler_params=_params("parallel", "arbitrary"),
        name="rwkv_mix",
    )(z, z, z, z, sh, sh, sh, sh, mu, mu, mu, mu,
      prm["w0"], prm["wwa"], prm["a0"], prm["g2"], prm["k_k"], prm["k_a"], prm["r_k"],
      prm["ln_w"], prm["ln_b"], s0_cat, *tables)
    return out, s_t


def _pack_wwa(w2, a2):
    top = jnp.concatenate([w2, jnp.zeros_like(w2)], axis=1)
    bot = jnp.concatenate([jnp.zeros_like(a2), a2], axis=1)
    return jnp.concatenate([top, bot], axis=0)


def _top_blocks(gate, n_iota, n_valid, n_blocks, axis=-1):
    neg = -jnp.inf
    valid = n_iota < n_valid
    g = jnp.where(valid, gate, neg)
    sel = jnp.zeros(gate.shape, F32)
    for _ in range(MOBA_TOPK):
        m = jnp.max(g, axis=axis, keepdims=True)
        idx = jnp.min(jnp.where(g == m, n_iota, n_blocks), axis=axis, keepdims=True)
        pick = n_iota == idx
        sel = jnp.where(jnp.logical_and(pick, valid), 1.0, sel)
        g = jnp.where(pick, neg, g)
    return sel


def _top_blocks_by_rank(gate):
    n_iota = lax.broadcasted_iota(jnp.int32, gate.shape, 1)
    rank = jnp.zeros(gate.shape, F32)
    for n in range(gate.shape[1]):
        c = gate[:, n:n + 1]
        beats = jnp.logical_or(c > gate, jnp.logical_and(c == gate, n < n_iota))
        rank = rank + jnp.where(beats, 1.0, 0.0)
    return jnp.where(rank < MOBA_TOPK, 1.0, 0.0)


def _moba_prompt_kernel(q_ref, k_ref, v_ref, o_ref, kbf_ref, vt_ref, kmean_ref, *, T):
    NB = T // MOBA_BLOCK
    QB = MOBA_BLOCK
    KG = MOBA_KEY_GROUP
    BPG = KG // MOBA_BLOCK
    NH = LANE // HEAD_DIM
    blk = pl.program_id(2)
    scale = HEAD_DIM ** -0.5 * LOG2E

    @pl.when(blk == 0)
    def _():
        kf = k_ref[...]
        kbf_ref[...] = kf.astype(BF16)
        kmean_ref[...] = jnp.sum(kf.reshape(NB, MOBA_BLOCK, LANE), axis=1) * (1.0 / MOBA_BLOCK)
        for g in range(T // KG):
            vt_ref[g] = v_ref[g * KG:(g + 1) * KG, :].T.astype(BF16)

    q = q_ref[...]
    lane = lax.broadcasted_iota(jnp.int32, (QB, LANE), 1)
    n_iota = lax.broadcasted_iota(jnp.int32, (NB, QB), 0)
    kmean = kmean_ref[...]
    qbs, sels = [], []
    for hh in range(NH):
        qm = jnp.where((lane // HEAD_DIM) == hh, q, 0.0)
        gate = lax.dot_general(kmean, qm, NT, precision=HI, preferred_element_type=F32)
        qbs.append((qm * scale).astype(BF16))
        sels.append(_top_blocks(gate, n_iota, blk, NB, axis=0))

    kidx = lax.broadcasted_iota(jnp.int32, (MOBA_BLOCK, QB), 0)
    qidx = lax.broadcasted_iota(jnp.int32, (MOBA_BLOCK, QB), 1)
    n_groups = (blk * QB) // KG + 1

    def step(gi, carry, own_group, nb=BPG):
        st = pl.multiple_of(gi * KG, KG)
        kg = kbf_ref[pl.ds(st, nb * MOBA_BLOCK), :]
        vtg = vt_ref[gi, :, :nb * MOBA_BLOCK]
        heads = range(NH)
        s = [lax.dot_general(kg, qbs[hh], NT, preferred_element_type=F32) for hh in heads]

        def mask(hh):
            parts = []
            for j in range(nb):
                n = gi * BPG + j
                ok = jnp.max(jnp.where(n_iota == n, sels[hh], 0.0), axis=0, keepdims=True) > 0.5
                if own_group:
                    ok = jnp.logical_or(ok, jnp.logical_and(n == blk, kidx <= qidx))
                parts.append(jnp.where(ok, s[hh][j * MOBA_BLOCK:(j + 1) * MOBA_BLOCK], -jnp.inf))
            return jnp.concatenate(parts, axis=0)

        s = [mask(hh) for hh in heads]
        m_new = [jnp.maximum(carry[hh][0], jnp.max(s[hh], axis=0, keepdims=True)) for hh in heads]
        alpha = [jnp.exp2(carry[hh][0] - m_new[hh]) for hh in heads]
        p = [jnp.exp2(s[hh] - m_new[hh]) for hh in heads]
        l = [alpha[hh] * carry[hh][1] + jnp.sum(p[hh], axis=0, keepdims=True) for hh in heads]
        pv = [jnp.dot(vtg, p[hh].astype(BF16), preferred_element_type=F32) for hh in heads]
        return tuple((m_new[hh], l[hh], alpha[hh] * carry[hh][2] + pv[hh]) for hh in heads)

    init = tuple((jnp.full((1, QB), -jnp.inf, F32), jnp.zeros((1, QB), F32), jnp.zeros((LANE, QB), F32))
                 for _ in range(NH))
    first = lax.switch(blk % BPG, [functools.partial(step, n_groups - 1, init, True, k + 1)
                                   for k in range(BPG)])
    res = lax.fori_loop(1, n_groups, lambda i, carry: step(n_groups - 1 - i, carry, False), first)
    rowi = lax.broadcasted_iota(jnp.int32, (LANE, QB), 0)
    out_t = jnp.zeros((LANE, QB), F32)
    for hh in range(NH):
        _, l, acc = res[hh]
        out_t = jnp.where((rowi // HEAD_DIM) == hh, acc / l, out_t)
    o_ref[...] = out_t.T


def moba_prompt(z, B, T):
    assert T % MOBA_KEY_GROUP == 0 and MOBA_BLOCK % MOBA_QCHUNK == 0
    NB = T // MOBA_BLOCK
    NKG = T // MOBA_KEY_GROUP
    P = MOBA_WIDTH // LANE
    q0 = RWKV_PROJ // LANE
    k0 = (RWKV_PROJ + MOBA_WIDTH) // LANE
    v0 = (RWKV_PROJ + 2 * MOBA_WIDTH) // LANE
    return pl.pallas_call(
        functools.partial(_moba_prompt_kernel, T=T),
        out_shape=jax.ShapeDtypeStruct((B * T, MOBA_WIDTH), F32),
        grid=(B, P, NB),
        in_specs=[pl.BlockSpec((MOBA_BLOCK, LANE), lambda b, p, c: (b * NB + c, q0 + p)),
                  pl.BlockSpec((T, LANE), lambda b, p, c: (b, k0 + p)),
                  pl.BlockSpec((T, LANE), lambda b, p, c: (b, v0 + p))],
        out_specs=pl.BlockSpec((MOBA_BLOCK, LANE), lambda b, p, c: (b * NB + c, p)),
        scratch_shapes=[pltpu.VMEM((T, LANE), BF16), pltpu.VMEM((NKG, LANE, MOBA_KEY_GROUP), BF16),
                        pltpu.VMEM((NB, LANE), F32)],
        compiler_params=_params("parallel", "parallel", "arbitrary"),
        name="moba_prompt",
    )(z, z, z)


def _moba_sample_kernel(pt_ref, q_ref, kn_ref, vn_ref, *refs, n_pages, tq):
    kp = refs[:n_pages]
    vp = refs[n_pages:2 * n_pages]
    hm_ref = refs[2 * n_pages]
    o_ref = refs[2 * n_pages + 1]
    H = MOBA_HEADS
    R = H * tq
    page = kp[0].shape[3]
    ppb = MOBA_BLOCK // page
    n_past = n_pages // ppb
    scale = HEAD_DIM ** -0.5 * LOG2E
    hm = hm_ref[...]
    qs = jnp.concatenate([q_ref[...]] * H, axis=0) * (hm * scale)
    qh, ql = _split2(qs)

    n_iota = lax.broadcasted_iota(jnp.int32, (R, n_past), 1)
    pages = range(n_pages)
    ks = [_split2(kp[j][0].reshape(MOBA_WIDTH, page)) for j in pages]
    s_hi = [jnp.dot(qh, ks[j][0], preferred_element_type=F32) for j in pages]
    s_x1 = [jnp.dot(qh, ks[j][1], preferred_element_type=F32) for j in pages]
    s_x2 = [jnp.dot(ql, ks[j][0], preferred_element_type=F32) for j in pages]
    fine = [s_hi[j] + (s_x1[j] + s_x2[j]) for j in pages]
    gate = jnp.zeros((R, n_past), F32)
    for n in range(n_past):
        blk_el = fine[n * ppb]
        for j in range(1, ppb):
            blk_el = blk_el + fine[n * ppb + j]
        gate = jnp.where(n_iota == n, jnp.sum(blk_el, axis=-1, keepdims=True), gate)
    sel = _top_blocks_by_rank(gate)

    scores = [jnp.where(sel[:, j // ppb:j // ppb + 1] > 0.5, s_hi[j], -jnp.inf) for j in range(n_pages)]
    s_own = lax.dot_general(qh, kn_ref[...].astype(BF16), NT, preferred_element_type=F32)
    qi = lax.broadcasted_iota(jnp.int32, (R, tq), 0) % tq
    ki = lax.broadcasted_iota(jnp.int32, (R, tq), 1)
    s_own = jnp.where(ki <= qi, s_own, -jnp.inf)

    m_el = scores[0]
    for j in range(1, n_pages):
        m_el = jnp.maximum(m_el, scores[j])
    m = jnp.maximum(jnp.max(s_own, axis=-1, keepdims=True), jnp.max(m_el, axis=-1, keepdims=True))
    p_own = jnp.exp2(s_own - m)
    acc = jnp.dot(p_own.astype(BF16), vn_ref[...].astype(BF16), preferred_element_type=F32)
    p = [jnp.exp2(scores[j] - m) for j in pages]
    vt = [vp[j][0].reshape(MOBA_WIDTH, page).astype(BF16) for j in pages]
    pv = [lax.dot_general(p[j].astype(BF16), vt[j], NT, preferred_element_type=F32) for j in pages]
    l_el = p[0]
    for j in range(1, n_pages):
        l_el = l_el + p[j]
        acc = acc + pv[j]
    acc = acc + pv[0]
    l = jnp.sum(p_own, axis=-1, keepdims=True) + jnp.sum(l_el, axis=-1, keepdims=True)
    om = (acc / l) * hm
    out = om[:tq]
    for h in range(1, H):
        out = out + om[h * tq:(h + 1) * tq]
    o_ref[...] = out


def moba_sample(zq, zk, zv, pool_k, pool_v, page_table, B, T):
    n_pages = page_table.shape[1]
    n_pool, page_size = pool_k.shape[0], pool_k.shape[1]
    assert MOBA_BLOCK % page_size == 0 and (n_pages * page_size) % MOBA_BLOCK == 0
    assert T <= MOBA_BLOCK and T % MOBA_QCHUNK != 0 and n_pages * page_size // MOBA_BLOCK >= MOBA_TOPK
    H = MOBA_HEADS
    R = H * T
    pk = pool_k.transpose(0, 2, 3, 1)
    pv = pool_v.transpose(0, 2, 3, 1)
    hm = jnp.asarray(((np.arange(R)[:, None] // T) == (np.arange(MOBA_WIDTH)[None, :] // HEAD_DIM))
                     .astype(np.float32))
    rows = pl.BlockSpec((T, MOBA_WIDTH), lambda b, pt: (b, 0))
    page = lambda j: pl.BlockSpec((1, H, HEAD_DIM, page_size), lambda b, pt: (pt[b, j], 0, 0, 0))
    return pl.pallas_call(
        functools.partial(_moba_sample_kernel, n_pages=n_pages, tq=T),
        out_shape=jax.ShapeDtypeStruct((B * T, MOBA_WIDTH), F32),
        grid_spec=pltpu.PrefetchScalarGridSpec(
            num_scalar_prefetch=1,
            grid=(B,),
            in_specs=[rows, rows, rows] + [page(j) for j in range(n_pages)] * 2
            + [pl.BlockSpec(hm.shape, lambda b, pt: (0, 0))],
            out_specs=rows),
        compiler_params=_params("parallel"),
        name="moba_sample",
    )(page_table, zq, zk, zv, *([pk] * n_pages), *([pv] * n_pages), hm)


def _retention_kernel(q_ref, k_ref, v_ref, g_ref, cos_ref, sin_ref, lg_ref, gnw_ref, s0_ref,
                      o_ref, s_out_ref, *, C):
    c = pl.program_id(1)

    @pl.when(c == 0)
    def _():
        s_out_ref[...] = s0_ref[...]

    half = RET_DK // 2
    heads = range(RET_HEADS)
    cos = cos_ref[...]
    sin = sin_ref[...]

    def rot(x_ref, h):
        x1 = x_ref[:, h * RET_DK:h * RET_DK + half]
        x2 = x_ref[:, h * RET_DK + half:(h + 1) * RET_DK]
        return jnp.concatenate([x1 * cos - x2 * sin, x1 * sin + x2 * cos], axis=-1)

    ii = lax.broadcasted_iota(jnp.int32, (C, C), 0)
    jj = lax.broadcasted_iota(jnp.int32, (C, C), 1)
    diff = jnp.maximum((ii - jj).astype(F32), 0.0)
    it = lax.broadcasted_iota(jnp.int32, (C, LANE), 0).astype(F32)
    lg = [lg_ref[h] for h in heads]
    dmask = [jnp.where(ii >= jj, jnp.exp(diff * lg[h][:, :C]), 0.0) for h in heads]
    cross = [jnp.exp((it + 1.0) * lg[h])[:, :1] for h in heads]
    kdec = [jnp.exp((C - 1.0 - it) * lg[h])[:, :1] for h in heads]
    sdec = [jnp.exp(C * lg[h])[:, :1] for h in heads]

    qb = [rot(q_ref, h).astype(BF16) for h in heads]
    kr = [rot(k_ref, h) * (RET_DK ** -0.5) for h in heads]
    v = [v_ref[:, h * RET_DV:(h + 1) * RET_DV] for h in heads]
    vb = [x.astype(BF16) for x in v]
    s = [s_out_ref[0, h] for h in heads]
    att = [lax.dot_general(qb[h], kr[h].astype(BF16), NT, preferred_element_type=F32) * dmask[h]
           for h in heads]
    y_in = [jnp.dot(att[h].astype(BF16), vb[h], preferred_element_type=F32) for h in heads]
    y_x = [jnp.dot(qb[h], s[h].astype(BF16), preferred_element_type=F32) * cross[h] for h in heads]
    for h in heads:
        s_out_ref[0, h] = s[h] * sdec[h] + lax.dot_general(kr[h] * kdec[h], v[h], TN,
                                                           preferred_element_type=F32)
    for h in heads:
        y = y_in[h] + y_x[h]
        mu = jnp.mean(y, axis=-1, keepdims=True)
        yc = y - mu
        sl = slice(h * RET_DV, (h + 1) * RET_DV)
        yn = yc * lax.rsqrt(jnp.mean(yc * yc, axis=-1, keepdims=True) + GN_EPS) * gnw_ref[:, sl]
        g = g_ref[:, sl]
        o_ref[:, sl] = (g * jax.nn.sigmoid(g)) * yn


def retention_mix(z, s0, gn_w, B, T, pos0):
    C = RET_CHUNK if T % RET_CHUNK == 0 else T
    NC = T // C
    half = RET_DK // 2
    pos = (pos0 + jnp.arange(T)).astype(F32)
    inv = ROPE_BASE ** (-jnp.arange(half, dtype=F32) / half)
    ang = pos[:, None] * inv[None, :]
    log_g = jnp.log1p(-jnp.exp2(-5.0 - jnp.arange(RET_HEADS, dtype=F32)))
    lg = jnp.broadcast_to(log_g[:, None, None], (RET_HEADS, 1, LANE))
    row = lambda b, c: b * NC + c
    state = pl.BlockSpec((1, RET_HEADS, RET_DK, RET_DV), lambda b, c: (b, 0, 0, 0))
    return pl.pallas_call(
        functools.partial(_retention_kernel, C=C),
        out_shape=(jax.ShapeDtypeStruct((B * T, RET_V), F32),
                   jax.ShapeDtypeStruct((B, RET_HEADS, RET_DK, RET_DV), F32)),
        grid=(B, NC),
        in_specs=[pl.BlockSpec((C, RET_QK), lambda b, c: (row(b, c), 0)),
                  pl.BlockSpec((C, RET_QK), lambda b, c: (row(b, c), 1)),
                  pl.BlockSpec((C, RET_V), lambda b, c: (row(b, c), 2 * RET_QK // RET_V)),
                  pl.BlockSpec((C, RET_V), lambda b, c: (row(b, c), 2 * RET_QK // RET_V + 1)),
                  pl.BlockSpec((C, half), lambda b, c: (c, 0)),
                  pl.BlockSpec((C, half), lambda b, c: (c, 0)),
                  pl.BlockSpec((RET_HEADS, 1, LANE), lambda b, c: (0, 0, 0)),
                  pl.BlockSpec((1, RET_V), lambda b, c: (0, 0)),
                  state],
        out_specs=(pl.BlockSpec((C, RET_V), lambda b, c: (row(b, c), 0)), state),
        compiler_params=_params("parallel", "arbitrary"),
        name="retention_mix",
    )(z, z, z, z, jnp.cos(ang), jnp.sin(ang), lg, gn_w.reshape(1, RET_V), s0)


def _run_trunk(x, p, B, T, pos0, moba_fn, rwkv_s0, rwkv_shift0, ret_s0, G, C, W, time_minor_kv):
    h = x
    if time_minor_kv:
        z, k_t, v_t = norm_matmul(h, W["norm_mix"][0], W["a_w_in"][0], tn=A_IN // 2,
                                  transposed=(W["a_w_kv_t"], 2, B, T))
        rows_out = lambda a: a.reshape(B, MOBA_HEADS, HEAD_DIM, T).transpose(0, 3, 1, 2)[None]
        k_rows, v_rows = rows_out(k_t), rows_out(v_t)
        zk = zv = None
    else:
        z = norm_matmul(h, W["norm_mix"][0], W["a_w_in"][0], tn=A_IN // 2)
        zk = z[:, RWKV_PROJ + MOBA_WIDTH:RWKV_PROJ + 2 * MOBA_WIDTH]
        zv = z[:, RWKV_PROJ + 2 * MOBA_WIDTH:]
        k_rows = zk.reshape(1, B, T, MOBA_HEADS, HEAD_DIM)
        v_rows = zv.reshape(1, B, T, MOBA_HEADS, HEAD_DIM)
    s0_cat = rwkv_s0.transpose(0, 2, 1, 3).reshape(B, HEAD_DIM, RWKV_WIDTH)
    rwkv_out, s_cat = rwkv_mix(z, rwkv_shift0, s0_cat, W["rwkv"], B, T, G, C)
    rwkv_state = s_cat.reshape(B, HEAD_DIM, RWKV_HEADS, HEAD_DIM).transpose(0, 2, 1, 3)
    moba_out = moba_fn(z, zk, zv)
    shift_t = z.reshape(B, T, A_IN)[:, T - 1, :RWKV_PROJ]
    wo = W["a_w_out"][0]
    h = proj_residual(h, [rwkv_out, moba_out], [wo[:RWKV_WIDTH], wo[RWKV_WIDTH:]])
    h = ffn_residual(h, W["norm_ffn"][0], W["ffn_w_gate"][0], W["ffn_w_up"][0], W["ffn_w_down"][0])
    h = ple_residual(h, W["ple_norm"][0], p[0], W["ple_proj"][0], W["ple_gate"][0],
                     W["final_norm"], final=False)
    z2 = norm_matmul(h, W["norm_mix"][1], W["c_w_in"][0], tn=C_IN // 3)
    gated, ret_state = retention_mix(z2, ret_s0, W["ret_gn_w"][0], B, T, pos0)
    h = proj_residual(h, [gated], [W["c_w_out"][0]])
    h = ffn_residual(h, W["norm_ffn"][1], W["ffn_w_gate"][1], W["ffn_w_up"][1], W["ffn_w_down"][1])
    y = ple_residual(h, W["ple_norm"][1], p[1], W["ple_proj"][1], W["ple_gate"][1],
                     W["final_norm"], final=True)
    return (y.reshape(B, T, D_MODEL), k_rows, v_rows, rwkv_state[None],
            shift_t[None], ret_state[None])


def kernel(x_prompt, x_sample, cache_moba_k, cache_moba_v, state_rwkv, state_rwkv_shift, state_ret, page_table, p_prompt, p_sample, norm_mix, norm_ffn, ffn_w_gate, ffn_w_up, ffn_w_down, ple_norm, ple_gate, ple_proj, a_w_in, rwkv_mu, rwkv_w0, rwkv_w2, rwkv_a0, rwkv_a2, rwkv_g2, rwkv_k_k, rwkv_k_a, rwkv_r_k, rwkv_ln_w, rwkv_ln_b, a_w_out, c_w_in, ret_gn_w, c_w_out, final_norm):
    assert norm_mix.shape[0] == DEPTH == 2
    B, T, _ = x_prompt.shape
    Bd, Td, _ = x_sample.shape
    bf = lambda w: w.astype(BF16)
    row = lambda a: a.reshape(1, RWKV_WIDTH)
    rwkv = dict(mu=rwkv_mu[0], w0=row(rwkv_w0[0]), a0=row(rwkv_a0[0]), g2=rwkv_g2[0],
                k_k=row(rwkv_k_k[0]), k_a=row(rwkv_k_a[0]), r_k=row(rwkv_r_k[0]),
                ln_w=row(rwkv_ln_w[0]), ln_b=row(rwkv_ln_b[0]),
                wwa=_pack_wwa(rwkv_w2[0], rwkv_a2[0]))
    W = dict(norm_mix=norm_mix, norm_ffn=norm_ffn, ffn_w_gate=bf(ffn_w_gate), ffn_w_up=bf(ffn_w_up),
             ffn_w_down=bf(ffn_w_down), ple_norm=ple_norm, ple_gate=bf(ple_gate), ple_proj=bf(ple_proj),
             a_w_in=bf(a_w_in), a_w_out=bf(a_w_out), c_w_in=bf(c_w_in), c_w_out=bf(c_w_out),
             a_w_kv_t=bf(a_w_in[0, :, RWKV_PROJ + MOBA_WIDTH:]).T,
             ret_gn_w=ret_gn_w, final_norm=final_norm, rwkv=rwkv)

    yp, kp, vp, rsp, shp, rtp = _run_trunk(
        x_prompt.reshape(B * T, D_MODEL), p_prompt.reshape(DEPTH, B * T, PLE_DIM), B, T, 0,
        lambda z, zk, zv: moba_prompt(z, B, T),
        jnp.zeros((B, RWKV_HEADS, HEAD_DIM, HEAD_DIM), F32), jnp.zeros((B, RWKV_PROJ), F32),
        jnp.zeros((B, RET_HEADS, RET_DK, RET_DV), F32), RWKV_GROUP, min(RWKV_CHUNK, T), W, True)

    past_len = page_table.shape[1] * cache_moba_k.shape[2]
    q_lo = RWKV_PROJ
    ys, ks, vs, rss, shs, rts = _run_trunk(
        x_sample.reshape(Bd * Td, D_MODEL), p_sample.reshape(DEPTH, Bd * Td, PLE_DIM), Bd, Td, past_len,
        lambda z, zk, zv: moba_sample(z[:, q_lo:q_lo + MOBA_WIDTH], zk, zv, cache_moba_k[0],
                                      cache_moba_v[0], page_table, Bd, Td),
        state_rwkv[0], state_rwkv_shift[0], state_ret[0], RWKV_GROUP, min(RWKV_CHUNK, Td), W, False)

    return (yp, ys, kp, vp, ks, vs, rsp, rss, shp, shs, rtp, rts)
```

```python
import functools
import math

import numpy as np
import jax
import jax.numpy as jnp
from jax import lax
from jax.experimental import pallas as pl
from jax.experimental.pallas import tpu as pltpu

F32 = jnp.float32
BF16 = jnp.bfloat16
HI = lax.Precision.HIGHEST

D_MODEL = 1024
DEPTH = 2
PLE_DIM = 256
HEAD_DIM = 64
RWKV_HEADS = 8
RWKV_WIDTH = RWKV_HEADS * HEAD_DIM
W_LORA = 64
A_LORA = 64
G_LORA = 128
RWKV_PROJ = 3 * RWKV_WIDTH + W_LORA + A_LORA + G_LORA
RWKV_LN_EPS = 64e-5
RWKV_CHUNK = 64
RWKV_GROUP = 2
RWKV_SOLVE_BLOCK = 8
MOBA_HEADS = 8
MOBA_WIDTH = MOBA_HEADS * HEAD_DIM
MOBA_BLOCK = 256
MOBA_TOPK = 3
MOBA_QCHUNK = 128
MOBA_KEY_GROUP = 4 * MOBA_BLOCK
A_IN = RWKV_PROJ + 3 * MOBA_WIDTH
RET_HEADS = 4
RET_DK = 256
RET_DV = 512
RET_QK = RET_HEADS * RET_DK
RET_V = RET_HEADS * RET_DV
C_IN = 2 * RET_QK + 2 * RET_V
RET_CHUNK = 128
ROPE_BASE = 10000.0
NORM_EPS = 1e-6
GN_EPS = 1e-5
LOG2E = math.log2(math.e)

LANE = 128
VMEM_LIMIT = 56 * 1024 * 1024

NN = (((1,), (0,)), ((), ()))
NT = (((1,), (1,)), ((), ()))
TN = (((0,), (0,)), ((), ()))


def _params(*sem):
    return pltpu.CompilerParams(dimension_semantics=sem, vmem_limit_bytes=VMEM_LIMIT)


def _rms(x, g):
    return x * lax.rsqrt(jnp.mean(x * x, axis=-1, keepdims=True) + NORM_EPS) * g


ROW_TILE = 1024
ROW_TILE_RESIDENT = 512
COL_TILE_MAX = 2048


def _row_tile(n, cap=ROW_TILE):
    t = cap
    while t >= 8:
        if n % t == 0:
            return t
        t //= 2
    raise ValueError(n)


def _col_tile(f):
    for k in range(1, f // LANE + 1):
        if f % k == 0 and (f // k) % LANE == 0 and f // k <= COL_TILE_MAX:
            return f // k
    raise ValueError(f)


def _norm_matmul_kernel(x_ref, g_ref, w_ref, *rest, n_t):
    if n_t:
        wt_ref, o_ref = rest[:2]
        t_refs = rest[2:2 + n_t]
    else:
        o_ref = rest[0]
    xn_ref = rest[-1]

    @pl.when(pl.program_id(1) == 0)
    def _():
        xn_ref[...] = _rms(x_ref[...], g_ref[...]).astype(BF16)

    o_ref[...] = jnp.dot(xn_ref[...], w_ref[...], preferred_element_type=F32)

    if n_t:
        @pl.when(pl.program_id(1) == pl.num_programs(1) - 1)
        def _():
            yt = lax.dot_general(wt_ref[...], xn_ref[...], NT, preferred_element_type=F32)
            cw = yt.shape[0] // n_t
            for k, t_ref in enumerate(t_refs):
                t_ref[0] = yt[k * cw:(k + 1) * cw]


def norm_matmul(x, g, w, transposed=None):
    n, d = x.shape
    f = w.shape[1]
    tm = _row_tile(n)
    tn = _col_tile(f)
    in_specs = [pl.BlockSpec((tm, d), lambda i, j: (i, 0)),
                pl.BlockSpec((1, d), lambda i, j: (0, 0)),
                pl.BlockSpec((d, tn), lambda i, j: (0, j))]
    out_shape = [jax.ShapeDtypeStruct((n, f), F32)]
    out_specs = [pl.BlockSpec((tm, tn), lambda i, j: (i, j))]
    args = [x, g.reshape(1, d), w]
    n_t = 0
    if transposed is not None:
        wt, n_t, b, t = transposed
        cw = wt.shape[0] // n_t
        nt = t // tm
        in_specs.append(pl.BlockSpec(wt.shape, lambda i, j: (0, 0)))
        args.append(wt)
        out_shape += [jax.ShapeDtypeStruct((b, cw, t), F32)] * n_t
        out_specs += [pl.BlockSpec((1, cw, tm), lambda i, j: (i // nt, 0, i % nt))] * n_t
    res = pl.pallas_call(
        functools.partial(_norm_matmul_kernel, n_t=n_t),
        out_shape=out_shape,
        grid=(n // tm, f // tn),
        in_specs=in_specs,
        out_specs=out_specs,
        scratch_shapes=[pltpu.VMEM((tm, d), BF16)],
        compiler_params=_params("parallel", "arbitrary"),
        name="norm_matmul",
    )(*args)
    return res if n_t else res[0]


def _proj_res_kernel(*refs, n_x):
    h_ref = refs[0]
    xs = refs[1:1 + n_x]
    ws = refs[1 + n_x:1 + 2 * n_x]
    o_ref = refs[1 + 2 * n_x]
    acc = h_ref[...]
    for x_ref, w_ref in zip(xs, ws):
        acc = acc + jnp.dot(x_ref[...].astype(BF16), w_ref[...], preferred_element_type=F32)
    o_ref[...] = acc


def proj_residual(h, xs, ws):
    n, d = h.shape
    tm = _row_tile(n, ROW_TILE_RESIDENT)
    in_specs = [pl.BlockSpec((tm, d), lambda i: (i, 0))]
    in_specs += [pl.BlockSpec((tm, x.shape[1]), lambda i: (i, 0)) for x in xs]
    in_specs += [pl.BlockSpec(w.shape, lambda i: (0, 0)) for w in ws]
    return pl.pallas_call(
        functools.partial(_proj_res_kernel, n_x=len(xs)),
        out_shape=jax.ShapeDtypeStruct((n, d), F32),
        grid=(n // tm,),
        in_specs=in_specs,
        out_specs=pl.BlockSpec((tm, d), lambda i: (i, 0)),
        compiler_params=_params("parallel"),
        name="proj_residual",
    )(h, *xs, *ws)


FFN_TILE = 256


def _ffn_kernel(h_ref, g_ref, wg_ref, wu_ref, wd_ref, o_ref, xn_ref):
    @pl.when(pl.program_id(1) == 0)
    def _():
        x = h_ref[...]
        xn_ref[...] = _rms(x, g_ref[...]).astype(BF16)
        o_ref[...] = x

    xn = xn_ref[...]
    a = jnp.dot(xn, wg_ref[...], preferred_element_type=F32)
    b = jnp.dot(xn, wu_ref[...], preferred_element_type=F32)
    t = (a * jax.nn.sigmoid(a)) * b
    o_ref[...] += jnp.dot(t.astype(BF16), wd_ref[...], preferred_element_type=F32)


def ffn_residual(h, g, wg, wu, wd):
    n, d = h.shape
    ff = wg.shape[1]
    tf = FFN_TILE
    tm = _row_tile(n)
    return pl.pallas_call(
        _ffn_kernel,
        out_shape=jax.ShapeDtypeStruct((n, d), F32),
        grid=(n // tm, ff // tf),
        in_specs=[pl.BlockSpec((tm, d), lambda i, j: (i, 0)),
                  pl.BlockSpec((1, d), lambda i, j: (0, 0)),
                  pl.BlockSpec((d, tf), lambda i, j: (0, j)),
                  pl.BlockSpec((d, tf), lambda i, j: (0, j)),
                  pl.BlockSpec((tf, d), lambda i, j: (j, 0))],
        out_specs=pl.BlockSpec((tm, d), lambda i, j: (i, 0)),
        scratch_shapes=[pltpu.VMEM((tm, d), BF16)],
        compiler_params=_params("parallel", "arbitrary"),
        name="ffn_residual",
    )(h, g.reshape(1, d), wg, wu, wd)


def _ple_kernel(h_ref, g_ref, p_ref, wp_ref, wgate_ref, fn_ref, o_ref, *, final):
    x = h_ref[...]
    xn = _rms(x, g_ref[...]).astype(BF16)
    gate = jax.nn.sigmoid(jnp.dot(xn, wgate_ref[...], preferred_element_type=F32))
    pp = jnp.dot(p_ref[...].astype(BF16), wp_ref[...], preferred_element_type=F32)
    hn = x + pp * gate
    if final:
        hn = _rms(hn, fn_ref[...])
    o_ref[...] = hn


def ple_residual(h, g, p, wp, wgate, final_g, final):
    n, d = h.shape
    pd = p.shape[1]
    tm = _row_tile(n, ROW_TILE_RESIDENT)
    return pl.pallas_call(
        functools.partial(_ple_kernel, final=final),
        out_shape=jax.ShapeDtypeStruct((n, d), F32),
        grid=(n // tm,),
        in_specs=[pl.BlockSpec((tm, d), lambda i: (i, 0)),
                  pl.BlockSpec((1, d), lambda i: (0, 0)),
                  pl.BlockSpec((tm, pd), lambda i: (i, 0)),
                  pl.BlockSpec((pd, d), lambda i: (0, 0)),
                  pl.BlockSpec((d, d), lambda i: (0, 0)),
                  pl.BlockSpec((1, d), lambda i: (0, 0))],
        out_specs=pl.BlockSpec((tm, d), lambda i: (i, 0)),
        compiler_params=_params("parallel"),
        name="ple_residual",
    )(h, g.reshape(1, d), p, wp, wgate, final_g.reshape(1, d))


def _softplus(x):
    return jnp.maximum(x, 0.0) + jnp.log1p(jnp.exp(-jnp.abs(x)))


def _split2(x):
    hi = x.astype(BF16)
    return hi, (x - hi.astype(F32)).astype(BF16)


def _split3(x):
    hi = x.astype(BF16)
    r1 = x - hi.astype(F32)
    mid = r1.astype(BF16)
    return hi, mid, (r1 - mid.astype(F32)).astype(BF16)


def _dot3(a, b, dims=NN):
    ah, al = _split2(a)
    bh, bl = _split2(b)
    d = lambda x, y: lax.dot_general(x, y, dims, preferred_element_type=F32)
    return d(ah, bh) + (d(ah, bl) + d(al, bh))


def _dot3_groups(xs, ys, dims=NN):
    return [_dot3(x, y, dims) for x, y in zip(xs, ys)]


def _dot_mask(a, m, mask_left=False):
    d = lambda x: lax.dot_general(m, x, NN, preferred_element_type=F32) if mask_left else \
        lax.dot_general(x, m, NN, preferred_element_type=F32)
    hi, mid, lo = _split3(a)
    return d(hi) + (d(mid) + d(lo))


def _rwkv_kernel(zr_ref, zk_ref, zv_ref, zl_ref, sr_ref, sk_ref, sv_ref, sl_ref,
                 mur_ref, muk_ref, muv_ref, mul_ref, w0_ref, wwa_ref, a0_ref, g2_ref,
                 kkw_ref, kaw_ref, rkw_ref, lnw_ref, lnb_ref, s0_ref,
                 hm_ref, mst_ref, min_ref, tri_ref, bd_ref, lvl_ref, eye_ref,
                 out_ref, s_out_ref, lr_ref, lk_ref, lv_ref, ll_ref, sbd_ref, *, G, C):
    L = G * HEAD_DIM
    R = G * C
    NG = RWKV_HEADS // G
    c = pl.program_id(1)
    wide = R % LANE == 0

    @pl.when(c == 0)
    def _():
        lr_ref[...] = sr_ref[0]
        lk_ref[...] = sk_ref[0]
        lv_ref[...] = sv_ref[0]
        ll_ref[...] = sl_ref[0]
        bdm = bd_ref[:L, :L].astype(F32)
        for gi in range(NG):
            sc = s0_ref[0, :, gi * L:(gi + 1) * L]
            sbd_ref[gi] = jnp.concatenate([sc] * G, axis=0) * bdm

    def shift_mix(x_ref, last_ref, mu_ref):
        x = x_ref[...]
        row = lax.broadcasted_iota(jnp.int32, x.shape, 0)
        prev = jnp.where(row == 0, last_ref[...], pltpu.roll(x, 1, axis=0))
        last_ref[...] = x[C - 1:C, :]
        return x + mu_ref[...] * (prev - x)

    r = shift_mix(zr_ref, lr_ref, mur_ref)
    k = shift_mix(zk_ref, lk_ref, muk_ref)
    v = shift_mix(zv_ref, lv_ref, muv_ref)
    lo = shift_mix(zl_ref, ll_ref, mul_ref)

    wa_in = lo[:, :W_LORA + A_LORA]
    lane = lax.broadcasted_iota(jnp.int32, wa_in.shape, 1)
    wa_in = jnp.where(lane < W_LORA, jnp.tanh(wa_in), wa_in)
    wa = jnp.dot(wa_in, wwa_ref[...], preferred_element_type=F32)
    logw = -_softplus(-(w0_ref[...] + wa[:, :RWKV_WIDTH])) - 0.5
    ld = -jnp.exp(logw)
    a = jax.nn.sigmoid(a0_ref[...] + wa[:, RWKV_WIDTH:])
    g = jnp.dot(jax.nn.sigmoid(lo[:, W_LORA + A_LORA:]), g2_ref[...], preferred_element_type=F32)

    kkv = k * kkw_ref[...]
    k2 = k * (1.0 + (a - 1.0) * kaw_ref[...])
    sums = _dot_mask(jnp.concatenate([kkv * kkv, r * k2 * rkw_ref[...]], axis=0), bd_ref[...])
    kkn = kkv / jnp.maximum(jnp.sqrt(sums[:C]), 1e-12)
    bonus = sums[C:] * v
    an = -kkn
    bn = kkn * a

    cs = _dot_mask(ld, tri_ref[...], mask_left=True)
    e_neg = jnp.exp(-cs)
    e_pos = jnp.exp(cs)
    at = an * jnp.exp(cs - ld)
    rt = r * e_pos
    bt = bn * e_neg
    kt = k2 * e_neg
    pc = e_pos[C - 1:C, :]

    hm = hm_ref[...]
    strict = mst_ref[...] > 0.5
    incl = min_ref[...] > 0.5
    groups = range(NG)
    sls = [slice(gi * L, (gi + 1) * L) for gi in groups]
    stack = lambda x, sl: jnp.concatenate([x[:, sl]] * G, axis=0) * hm
    a_s = [stack(at, sl) for sl in sls]
    r_s = [stack(rt, sl) for sl in sls]
    b_s = [stack(bt, sl) for sl in sls]
    k_s = [stack(kt, sl) for sl in sls]
    v_s = [stack(v, sl) for sl in sls]
    bk = [jnp.concatenate([b_s[i], k_s[i]], axis=0) for i in groups]
    lhs = [jnp.concatenate([a_s[i], r_s[i]], axis=0) for i in groups]
    gdot = _dot3_groups
    if wide:
        m = gdot(lhs, bk, NT)
        m_b = [x[:, :R] for x in m]
        m_k = [x[:, R:] for x in m]
    else:
        m_b, m_k = gdot(lhs, b_s, NT), gdot(lhs, k_s, NT)
    ma_b = [x[:R] for x in m_b]
    a_ak = [jnp.where(strict, x[:R], 0.0) for x in m_k]
    a_rb = [jnp.where(incl, x[R:], 0.0) for x in m_b]
    a_rk = [jnp.where(incl, x[R:], 0.0) for x in m_k]

    sb = [sbd_ref[i] for i in groups]
    w0s = gdot(lhs, sb, NT)
    w0r = [x[R:] for x in w0s]
    akv = gdot(a_ak, v_s)
    rhs = [w0s[i][:R] + akv[i] for i in groups]

    band = lambda i, lv: jnp.where(lvl_ref[lv] > 0.5, ma_b[i], 0.0)
    d = [band(i, 0) for i in groups]
    x = [eye_ref[...] + d[i] for i in groups]
    p2 = gdot(d, d)
    if wide:
        t = [_dot3(p2[i], jnp.concatenate([x[i], p2[i]], axis=1)) for i in groups]
        x = [x[i] + t[i][:, :R] for i in groups]
        p4 = [t[i][:, R:] for i in groups]
    else:
        x, p4 = [x[i] + _dot3(p2[i], x[i]) for i in groups], [_dot3(p2[i], p2[i]) for i in groups]
    p4x = gdot(p4, x)
    tinv = [x[i] + p4x[i] for i in groups]
    n_lv = lvl_ref.shape[0]
    for lv in range(1, n_lv - 1):
        nt = gdot([band(i, lv) for i in groups], tinv)
        tnt = gdot(tinv, nt)
        tinv = [tinv[i] + tnt[i] for i in groups]
    if n_lv == 1:
        u = gdot(tinv, rhs)
    else:
        if wide:
            vb = [_dot3(tinv[i], jnp.concatenate([rhs[i], band(i, n_lv - 1)], axis=1)) for i in groups]
            v0 = [y[:, :L] for y in vb]
            bl = [y[:, L:] for y in vb]
        else:
            v0 = [_dot3(tinv[i], rhs[i]) for i in groups]
            bl = [_dot3(tinv[i], band(i, n_lv - 1)) for i in groups]
        blv = gdot(bl, v0)
        u = [v0[i] + blv[i] for i in groups]
    if wide:
        ruv = gdot([jnp.concatenate([a_rb[i], a_rk[i]], axis=1) for i in groups],
                   [jnp.concatenate([u[i], v_s[i]], axis=0) for i in groups])
        y = [w0r[i] + ruv[i] for i in groups]
    else:
        ru, rv = gdot(a_rb, u), gdot(a_rk, v_s)
        y = [w0r[i] + ru[i] + rv[i] for i in groups]

    ys = []
    for i in groups:
        mu_y = jnp.sum(y[i], axis=-1, keepdims=True) * (1.0 / HEAD_DIM)
        yc = (y[i] - mu_y) * hm
        yn = yc * lax.rsqrt(jnp.sum(yc * yc, axis=-1, keepdims=True) * (1.0 / HEAD_DIM) + RWKV_LN_EPS)
        y_cl = yn[:C]
        for h in range(1, G):
            y_cl = y_cl + yn[h * C:(h + 1) * C]
        ys.append(y_cl)

    uv = [jnp.concatenate([u[i], v_s[i]], axis=0) for i in groups]
    s_new = [(sb[i] + _dot3(uv[i], bk[i], TN)) * pc[:, sls[i]] for i in groups]
    for i in groups:
        sbd_ref[i] = s_new[i]
        s_cat = s_new[i][:HEAD_DIM]
        for h in range(1, G):
            s_cat = s_cat + s_new[i][h * HEAD_DIM:(h + 1) * HEAD_DIM]
        s_out_ref[0, :, sls[i]] = s_cat
    y_all = ys[0] if NG == 1 else jnp.concatenate(ys, axis=1)
    out_ref[...] = (y_all * lnw_ref[...] + lnb_ref[...] + bonus) * g


def _rwkv_tables(G, C):
    L, R = G * HEAD_DIM, G * C
    i = np.arange(R)[:, None]
    j = np.arange(R)[None, :]
    l = np.arange(RWKV_WIDTH)
    same = (i // C) == (j // C)
    hm = ((np.arange(R)[:, None] // C) == (np.arange(L)[None, :] // HEAD_DIM))
    tri = np.arange(C)[:, None] >= np.arange(C)[None, :]
    bd = (l[:, None] // HEAD_DIM) == (l[None, :] // HEAD_DIM)
    f = lambda m, dt=np.float32: jnp.asarray(m.astype(np.float32), dtype=dt)
    assert C % RWKV_SOLVE_BLOCK == 0 and (C // RWKV_SOLVE_BLOCK) & (C // RWKV_SOLVE_BLOCK - 1) == 0
    levels, b = [], RWKV_SOLVE_BLOCK
    prev = np.zeros((R, R), bool)
    while b <= C:
        cur = ((i // b) == (j // b)) & (i > j)
        levels.append(cur & ~prev)
        prev, b = cur, 2 * b
    return [f(hm), f(same & (i > j)), f(same & (i >= j)), f(tri, BF16), f(bd, BF16),
            f(np.stack(levels)), f(np.eye(R, dtype=bool))]


def rwkv_mix(z, shift0, s0_cat, prm, B, T, G, C):
    L = G * HEAD_DIM
    NG = RWKV_HEADS // G
    NC = T // C
    W = RWKV_WIDTH
    lw = W_LORA + A_LORA + G_LORA
    lb = (3 * W) // lw
    row = lambda b, c: b * NC + c
    zspec = lambda off: pl.BlockSpec((C, W), lambda b, c: (row(b, c), off))
    sspec = lambda off: pl.BlockSpec((1, 1, W), lambda b, c: (b, 0, off))
    vspec = lambda off: pl.BlockSpec((1, W), lambda b, c: (0, off))
    const = lambda a: pl.BlockSpec(a.shape, lambda b, c: (0,) * a.ndim)
    tables = _rwkv_tables(G, C)
    mu = prm["mu"].reshape(1, RWKV_PROJ)
    sh = shift0.reshape(B, 1, RWKV_PROJ)
    in_specs = [zspec(0), zspec(1), zspec(2),
                pl.BlockSpec((C, lw), lambda b, c: (row(b, c), lb)),
                sspec(0), sspec(1), sspec(2),
                pl.BlockSpec((1, 1, lw), lambda b, c: (b, 0, lb)),
                vspec(0), vspec(1), vspec(2),
                pl.BlockSpec((1, lw), lambda b, c: (0, lb)),
                vspec(0),
                const(prm["wwa"]),
                vspec(0),
                const(prm["g2"]),
                vspec(0), vspec(0), vspec(0), vspec(0), vspec(0),
                pl.BlockSpec((1, HEAD_DIM, W), lambda b, c: (b, 0, 0))]
    in_specs += [const(t) for t in tables]
    out, s_t = pl.pallas_call(
        functools.partial(_rwkv_kernel, G=G, C=C),
        out_shape=(jax.ShapeDtypeStruct((B * T, W), F32),
                   jax.ShapeDtypeStruct((B, HEAD_DIM, W), F32)),
        grid=(B, NC),
        in_specs=in_specs,
        out_specs=(pl.BlockSpec((C, W), lambda b, c: (row(b, c), 0)),
                   pl.BlockSpec((1, HEAD_DIM, W), lambda b, c: (b, 0, 0))),
        scratch_shapes=[pltpu.VMEM((1, W), F32), pltpu.VMEM((1, W), F32),
                        pltpu.VMEM((1, W), F32), pltpu.VMEM((1, lw), F32),
                        pltpu.VMEM((NG, L, L), F32)],
        compiler_params=_params("parallel", "arbitrary"),
        name="rwkv_mix",
    )(z, z, z, z, sh, sh, sh, sh, mu, mu, mu, mu,
      prm["w0"], prm["wwa"], prm["a0"], prm["g2"], prm["k_k"], prm["k_a"], prm["r_k"],
      prm["ln_w"], prm["ln_b"], s0_cat, *tables)
    return out, s_t


def _pack_wwa(w2, a2):
    top = jnp.concatenate([w2, jnp.zeros_like(w2)], axis=1)
    bot = jnp.concatenate([jnp.zeros_like(a2), a2], axis=1)
    return jnp.concatenate([top, bot], axis=0)


def _top_blocks(gate, n_iota, n_valid, n_blocks, axis=-1):
    neg = -jnp.inf
    valid = n_iota < n_valid
    g = jnp.where(valid, gate, neg)
    sel = jnp.zeros(gate.shape, F32)
    for _ in range(MOBA_TOPK):
        m = jnp.max(g, axis=axis, keepdims=True)
        idx = jnp.min(jnp.where(g == m, n_iota, n_blocks), axis=axis, keepdims=True)
        pick = n_iota == idx
        sel = jnp.where(jnp.logical_and(pick, valid), 1.0, sel)
        g = jnp.where(pick, neg, g)
    return sel


def _top_blocks_by_rank(gate):
    n_iota = lax.broadcasted_iota(jnp.int32, gate.shape, 1)
    rank = jnp.zeros(gate.shape, F32)
    for n in range(gate.shape[1]):
        c = gate[:, n:n + 1]
        beats = jnp.logical_or(c > gate, jnp.logical_and(c == gate, n < n_iota))
        rank = rank + jnp.where(beats, 1.0, 0.0)
    return jnp.where(rank < MOBA_TOPK, 1.0, 0.0)


def _moba_prompt_kernel(q_ref, k_ref, v_ref, o_ref, kbf_ref, vt_ref, kmean_ref, *, T):
    NB = T // MOBA_BLOCK
    QB = MOBA_BLOCK
    KG = MOBA_KEY_GROUP
    BPG = KG // MOBA_BLOCK
    NH = LANE // HEAD_DIM
    blk = pl.program_id(2)
    scale = HEAD_DIM ** -0.5 * LOG2E

    @pl.when(blk == 0)
    def _():
        kf = k_ref[...]
        kbf_ref[...] = kf.astype(BF16)
        kmean_ref[...] = jnp.sum(kf.reshape(NB, MOBA_BLOCK, LANE), axis=1) * (1.0 / MOBA_BLOCK)
        for g in range(T // KG):
            vt_ref[g] = v_ref[g * KG:(g + 1) * KG, :].T.astype(BF16)

    q = q_ref[...]
    lane = lax.broadcasted_iota(jnp.int32, (QB, LANE), 1)
    n_iota = lax.broadcasted_iota(jnp.int32, (NB, QB), 0)
    kmean = kmean_ref[...]
    qbs, sels = [], []
    for hh in range(NH):
        qm = jnp.where((lane // HEAD_DIM) == hh, q, 0.0)
        gate = lax.dot_general(kmean, qm, NT, precision=HI, preferred_element_type=F32)
        qbs.append((qm * scale).astype(BF16))
        sels.append(_top_blocks(gate, n_iota, blk, NB, axis=0))

    kidx = lax.broadcasted_iota(jnp.int32, (MOBA_BLOCK, QB), 0)
    qidx = lax.broadcasted_iota(jnp.int32, (MOBA_BLOCK, QB), 1)
    n_groups = (blk * QB) // KG + 1

    def step(gi, carry, own_group):
        st = pl.multiple_of(gi * KG, KG)
        kg = kbf_ref[pl.ds(st, KG), :]
        vtg = vt_ref[gi]
        heads = range(NH)
        s = [lax.dot_general(kg, qbs[hh], NT, preferred_element_type=F32) for hh in heads]

        def mask(hh):
            parts = []
            for j in range(BPG):
                n = gi * BPG + j
                ok = jnp.max(jnp.where(n_iota == n, sels[hh], 0.0), axis=0, keepdims=True) > 0.5
                if own_group:
                    ok = jnp.logical_or(ok, jnp.logical_and(n == blk, kidx <= qidx))
                parts.append(jnp.where(ok, s[hh][j * MOBA_BLOCK:(j + 1) * MOBA_BLOCK], -jnp.inf))
            return jnp.concatenate(parts, axis=0)

        s = [mask(hh) for hh in heads]
        m_new = [jnp.maximum(carry[hh][0], jnp.max(s[hh], axis=0, keepdims=True)) for hh in heads]
        alpha = [jnp.exp2(carry[hh][0] - m_new[hh]) for hh in heads]
        p = [jnp.exp2(s[hh] - m_new[hh]) for hh in heads]
        l = [alpha[hh] * carry[hh][1] + jnp.sum(p[hh], axis=0, keepdims=True) for hh in heads]
        pv = [jnp.dot(vtg, p[hh].astype(BF16), preferred_element_type=F32) for hh in heads]
        return tuple((m_new[hh], l[hh], alpha[hh] * carry[hh][2] + pv[hh]) for hh in heads)

    init = tuple((jnp.full((1, QB), -jnp.inf, F32), jnp.zeros((1, QB), F32), jnp.zeros((LANE, QB), F32))
                 for _ in range(NH))
    first = step(n_groups - 1, init, True)
    res = lax.fori_loop(1, n_groups, lambda i, carry: step(n_groups - 1 - i, carry, False), first)
    rowi = lax.broadcasted_iota(jnp.int32, (LANE, QB), 0)
    out_t = jnp.zeros((LANE, QB), F32)
    for hh in range(NH):
        _, l, acc = res[hh]
        out_t = jnp.where((rowi // HEAD_DIM) == hh, acc / l, out_t)
    o_ref[...] = out_t.T


def moba_prompt(z, B, T):
    assert T % MOBA_KEY_GROUP == 0 and MOBA_BLOCK % MOBA_QCHUNK == 0
    NB = T // MOBA_BLOCK
    NKG = T // MOBA_KEY_GROUP
    P = MOBA_WIDTH // LANE
    q0 = RWKV_PROJ // LANE
    k0 = (RWKV_PROJ + MOBA_WIDTH) // LANE
    v0 = (RWKV_PROJ + 2 * MOBA_WIDTH) // LANE
    return pl.pallas_call(
        functools.partial(_moba_prompt_kernel, T=T),
        out_shape=jax.ShapeDtypeStruct((B * T, MOBA_WIDTH), F32),
        grid=(B, P, NB),
        in_specs=[pl.BlockSpec((MOBA_BLOCK, LANE), lambda b, p, c: (b * NB + c, q0 + p)),
                  pl.BlockSpec((T, LANE), lambda b, p, c: (b, k0 + p)),
                  pl.BlockSpec((T, LANE), lambda b, p, c: (b, v0 + p))],
        out_specs=pl.BlockSpec((MOBA_BLOCK, LANE), lambda b, p, c: (b * NB + c, p)),
        scratch_shapes=[pltpu.VMEM((T, LANE), BF16), pltpu.VMEM((NKG, LANE, MOBA_KEY_GROUP), BF16),
                        pltpu.VMEM((NB, LANE), F32)],
        compiler_params=_params("parallel", "parallel", "arbitrary"),
        name="moba_prompt",
    )(z, z, z)


def _moba_sample_kernel(pt_ref, q_ref, kn_ref, vn_ref, *refs, n_pages, tq):
    kp = refs[:n_pages]
    vp = refs[n_pages:2 * n_pages]
    hm_ref = refs[2 * n_pages]
    o_ref = refs[2 * n_pages + 1]
    H = MOBA_HEADS
    R = H * tq
    page = kp[0].shape[3]
    ppb = MOBA_BLOCK // page
    n_past = n_pages // ppb
    scale = HEAD_DIM ** -0.5 * LOG2E
    hm = hm_ref[...]
    qs = jnp.concatenate([q_ref[...]] * H, axis=0) * (hm * scale)
    qh, ql = _split2(qs)

    n_iota = lax.broadcasted_iota(jnp.int32, (R, n_past), 1)
    pages = range(n_pages)
    ks = [_split2(kp[j][0].reshape(MOBA_WIDTH, page)) for j in pages]
    s_hi = [jnp.dot(qh, ks[j][0], preferred_element_type=F32) for j in pages]
    s_x1 = [jnp.dot(qh, ks[j][1], preferred_element_type=F32) for j in pages]
    s_x2 = [jnp.dot(ql, ks[j][0], preferred_element_type=F32) for j in pages]
    fine = [s_hi[j] + (s_x1[j] + s_x2[j]) for j in pages]
    gate = jnp.zeros((R, n_past), F32)
    for n in range(n_past):
        blk_el = fine[n * ppb]
        for j in range(1, ppb):
            blk_el = blk_el + fine[n * ppb + j]
        gate = jnp.where(n_iota == n, jnp.sum(blk_el, axis=-1, keepdims=True), gate)
    sel = _top_blocks_by_rank(gate)

    scores = [jnp.where(sel[:, j // ppb:j // ppb + 1] > 0.5, s_hi[j], -jnp.inf) for j in range(n_pages)]
    s_own = lax.dot_general(qh, kn_ref[...].astype(BF16), NT, preferred_element_type=F32)
    qi = lax.broadcasted_iota(jnp.int32, (R, tq), 0) % tq
    ki = lax.broadcasted_iota(jnp.int32, (R, tq), 1)
    s_own = jnp.where(ki <= qi, s_own, -jnp.inf)

    m_el = scores[0]
    for j in range(1, n_pages):
        m_el = jnp.maximum(m_el, scores[j])
    m = jnp.maximum(jnp.max(s_own, axis=-1, keepdims=True), jnp.max(m_el, axis=-1, keepdims=True))
    p_own = jnp.exp2(s_own - m)
    acc = jnp.dot(p_own.astype(BF16), vn_ref[...].astype(BF16), preferred_element_type=F32)
    p = [jnp.exp2(scores[j] - m) for j in pages]
    vt = [vp[j][0].reshape(MOBA_WIDTH, page).astype(BF16) for j in pages]
    pv = [lax.dot_general(p[j].astype(BF16), vt[j], NT, preferred_element_type=F32) for j in pages]
    l_el = p[0]
    for j in range(1, n_pages):
        l_el = l_el + p[j]
        acc = acc + pv[j]
    acc = acc + pv[0]
    l = jnp.sum(p_own, axis=-1, keepdims=True) + jnp.sum(l_el, axis=-1, keepdims=True)
    om = (acc / l) * hm
    out = om[:tq]
    for h in range(1, H):
        out = out + om[h * tq:(h + 1) * tq]
    o_ref[...] = out


def moba_sample(zq, zk, zv, pool_k, pool_v, page_table, B, T):
    n_pages = page_table.shape[1]
    n_pool, page_size = pool_k.shape[0], pool_k.shape[1]
    assert MOBA_BLOCK % page_size == 0 and (n_pages * page_size) % MOBA_BLOCK == 0
    assert T <= MOBA_BLOCK and T % MOBA_QCHUNK != 0 and n_pages * page_size // MOBA_BLOCK >= MOBA_TOPK
    H = MOBA_HEADS
    R = H * T
    pk = pool_k.transpose(0, 2, 3, 1)
    pv = pool_v.transpose(0, 2, 3, 1)
    hm = jnp.asarray(((np.arange(R)[:, None] // T) == (np.arange(MOBA_WIDTH)[None, :] // HEAD_DIM))
                     .astype(np.float32))
    rows = pl.BlockSpec((T, MOBA_WIDTH), lambda b, pt: (b, 0))
    page = lambda j: pl.BlockSpec((1, H, HEAD_DIM, page_size), lambda b, pt: (pt[b, j], 0, 0, 0))
    return pl.pallas_call(
        functools.partial(_moba_sample_kernel, n_pages=n_pages, tq=T),
        out_shape=jax.ShapeDtypeStruct((B * T, MOBA_WIDTH), F32),
        grid_spec=pltpu.PrefetchScalarGridSpec(
            num_scalar_prefetch=1,
            grid=(B,),
            in_specs=[rows, rows, rows] + [page(j) for j in range(n_pages)] * 2
            + [pl.BlockSpec(hm.shape, lambda b, pt: (0, 0))],
            out_specs=rows),
        compiler_params=_params("parallel"),
        name="moba_sample",
    )(page_table, zq, zk, zv, *([pk] * n_pages), *([pv] * n_pages), hm)


def _retention_kernel(q_ref, k_ref, v_ref, g_ref, cos_ref, sin_ref, lg_ref, gnw_ref, s0_ref,
                      o_ref, s_out_ref, *, C):
    c = pl.program_id(1)

    @pl.when(c == 0)
    def _():
        s_out_ref[...] = s0_ref[...]

    half = RET_DK // 2
    heads = range(RET_HEADS)
    cos = cos_ref[...]
    sin = sin_ref[...]

    def rot(x_ref, h):
        x1 = x_ref[:, h * RET_DK:h * RET_DK + half]
        x2 = x_ref[:, h * RET_DK + half:(h + 1) * RET_DK]
        return jnp.concatenate([x1 * cos - x2 * sin, x1 * sin + x2 * cos], axis=-1)

    ii = lax.broadcasted_iota(jnp.int32, (C, C), 0)
    jj = lax.broadcasted_iota(jnp.int32, (C, C), 1)
    diff = jnp.maximum((ii - jj).astype(F32), 0.0)
    it = lax.broadcasted_iota(jnp.int32, (C, LANE), 0).astype(F32)
    lg = [lg_ref[h] for h in heads]
    dmask = [jnp.where(ii >= jj, jnp.exp(diff * lg[h][:, :C]), 0.0) for h in heads]
    cross = [jnp.exp((it + 1.0) * lg[h])[:, :1] for h in heads]
    kdec = [jnp.exp((C - 1.0 - it) * lg[h])[:, :1] for h in heads]
    sdec = [jnp.exp(C * lg[h])[:, :1] for h in heads]

    qb = [rot(q_ref, h).astype(BF16) for h in heads]
    kr = [rot(k_ref, h) * (RET_DK ** -0.5) for h in heads]
    v = [v_ref[:, h * RET_DV:(h + 1) * RET_DV] for h in heads]
    vb = [x.astype(BF16) for x in v]
    s = [s_out_ref[0, h] for h in heads]
    att = [lax.dot_general(qb[h], kr[h].astype(BF16), NT, preferred_element_type=F32) * dmask[h]
           for h in heads]
    y_in = [jnp.dot(att[h].astype(BF16), vb[h], preferred_element_type=F32) for h in heads]
    y_x = [jnp.dot(qb[h], s[h].astype(BF16), preferred_element_type=F32) * cross[h] for h in heads]
    for h in heads:
        s_out_ref[0, h] = s[h] * sdec[h] + lax.dot_general(kr[h] * kdec[h], v[h], TN,
                                                           preferred_element_type=F32)
    for h in heads:
        y = y_in[h] + y_x[h]
        mu = jnp.mean(y, axis=-1, keepdims=True)
        yc = y - mu
        sl = slice(h * RET_DV, (h + 1) * RET_DV)
        yn = yc * lax.rsqrt(jnp.mean(yc * yc, axis=-1, keepdims=True) + GN_EPS) * gnw_ref[:, sl]
        g = g_ref[:, sl]
        o_ref[:, sl] = (g * jax.nn.sigmoid(g)) * yn


def retention_mix(z, s0, gn_w, B, T, pos0):
    C = RET_CHUNK if T % RET_CHUNK == 0 else T
    NC = T // C
    half = RET_DK // 2
    pos = (pos0 + jnp.arange(T)).astype(F32)
    inv = ROPE_BASE ** (-jnp.arange(half, dtype=F32) / half)
    ang = pos[:, None] * inv[None, :]
    log_g = jnp.log1p(-jnp.exp2(-5.0 - jnp.arange(RET_HEADS, dtype=F32)))
    lg = jnp.broadcast_to(log_g[:, None, None], (RET_HEADS, 1, LANE))
    row = lambda b, c: b * NC + c
    state = pl.BlockSpec((1, RET_HEADS, RET_DK, RET_DV), lambda b, c: (b, 0, 0, 0))
    return pl.pallas_call(
        functools.partial(_retention_kernel, C=C),
        out_shape=(jax.ShapeDtypeStruct((B * T, RET_V), F32),
                   jax.ShapeDtypeStruct((B, RET_HEADS, RET_DK, RET_DV), F32)),
        grid=(B, NC),
        in_specs=[pl.BlockSpec((C, RET_QK), lambda b, c: (row(b, c), 0)),
                  pl.BlockSpec((C, RET_QK), lambda b, c: (row(b, c), 1)),
                  pl.BlockSpec((C, RET_V), lambda b, c: (row(b, c), 2 * RET_QK // RET_V)),
                  pl.BlockSpec((C, RET_V), lambda b, c: (row(b, c), 2 * RET_QK // RET_V + 1)),
                  pl.BlockSpec((C, half), lambda b, c: (c, 0)),
                  pl.BlockSpec((C, half), lambda b, c: (c, 0)),
                  pl.BlockSpec((RET_HEADS, 1, LANE), lambda b, c: (0, 0, 0)),
                  pl.BlockSpec((1, RET_V), lambda b, c: (0, 0)),
                  state],
        out_specs=(pl.BlockSpec((C, RET_V), lambda b, c: (row(b, c), 0)), state),
        compiler_params=_params("parallel", "arbitrary"),
        name="retention_mix",
    )(z, z, z, z, jnp.cos(ang), jnp.sin(ang), lg, gn_w.reshape(1, RET_V), s0)


def _run_trunk(x, p, B, T, pos0, moba_fn, rwkv_s0, rwkv_shift0, ret_s0, G, C, W, time_minor_kv):
    h = x
    if time_minor_kv:
        z, k_t, v_t = norm_matmul(h, W["norm_mix"][0], W["a_w_in"][0],
                                  transposed=(W["a_w_kv_t"], 2, B, T))
        rows_out = lambda a: a.reshape(B, MOBA_HEADS, HEAD_DIM, T).transpose(0, 3, 1, 2)[None]
        k_rows, v_rows = rows_out(k_t), rows_out(v_t)
        zk = zv = None
    else:
        z = norm_matmul(h, W["norm_mix"][0], W["a_w_in"][0])
        zk = z[:, RWKV_PROJ + MOBA_WIDTH:RWKV_PROJ + 2 * MOBA_WIDTH]
        zv = z[:, RWKV_PROJ + 2 * MOBA_WIDTH:]
        k_rows = zk.reshape(1, B, T, MOBA_HEADS, HEAD_DIM)
        v_rows = zv.reshape(1, B, T, MOBA_HEADS, HEAD_DIM)
    s0_cat = rwkv_s0.transpose(0, 2, 1, 3).reshape(B, HEAD_DIM, RWKV_WIDTH)
    rwkv_out, s_cat = rwkv_mix(z, rwkv_shift0, s0_cat, W["rwkv"], B, T, G, C)
    rwkv_state = s_cat.reshape(B, HEAD_DIM, RWKV_HEADS, HEAD_DIM).transpose(0, 2, 1, 3)
    moba_out = moba_fn(z, zk, zv)
    shift_t = z.reshape(B, T, A_IN)[:, T - 1, :RWKV_PROJ]
    wo = W["a_w_out"][0]
    h = proj_residual(h, [rwkv_out, moba_out], [wo[:RWKV_WIDTH], wo[RWKV_WIDTH:]])
    h = ffn_residual(h, W["norm_ffn"][0], W["ffn_w_gate"][0], W["ffn_w_up"][0], W["ffn_w_down"][0])
    h = ple_residual(h, W["ple_norm"][0], p[0], W["ple_proj"][0], W["ple_gate"][0],
                     W["final_norm"], final=False)
    z2 = norm_matmul(h, W["norm_mix"][1], W["c_w_in"][0])
    gated, ret_state = retention_mix(z2, ret_s0, W["ret_gn_w"][0], B, T, pos0)
    h = proj_residual(h, [gated], [W["c_w_out"][0]])
    h = ffn_residual(h, W["norm_ffn"][1], W["ffn_w_gate"][1], W["ffn_w_up"][1], W["ffn_w_down"][1])
    y = ple_residual(h, W["ple_norm"][1], p[1], W["ple_proj"][1], W["ple_gate"][1],
                     W["final_norm"], final=True)
    return (y.reshape(B, T, D_MODEL), k_rows, v_rows, rwkv_state[None],
            shift_t[None], ret_state[None])


def kernel(x_prompt, x_sample, cache_moba_k, cache_moba_v, state_rwkv, state_rwkv_shift, state_ret, page_table, p_prompt, p_sample, norm_mix, norm_ffn, ffn_w_gate, ffn_w_up, ffn_w_down, ple_norm, ple_gate, ple_proj, a_w_in, rwkv_mu, rwkv_w0, rwkv_w2, rwkv_a0, rwkv_a2, rwkv_g2, rwkv_k_k, rwkv_k_a, rwkv_r_k, rwkv_ln_w, rwkv_ln_b, a_w_out, c_w_in, ret_gn_w, c_w_out, final_norm):
    assert norm_mix.shape[0] == DEPTH == 2
    B, T, _ = x_prompt.shape
    Bd, Td, _ = x_sample.shape
    bf = lambda w: w.astype(BF16)
    row = lambda a: a.reshape(1, RWKV_WIDTH)
    rwkv = dict(mu=rwkv_mu[0], w0=row(rwkv_w0[0]), a0=row(rwkv_a0[0]), g2=rwkv_g2[0],
                k_k=row(rwkv_k_k[0]), k_a=row(rwkv_k_a[0]), r_k=row(rwkv_r_k[0]),
                ln_w=row(rwkv_ln_w[0]), ln_b=row(rwkv_ln_b[0]),
                wwa=_pack_wwa(rwkv_w2[0], rwkv_a2[0]))
    W = dict(norm_mix=norm_mix, norm_ffn=norm_ffn, ffn_w_gate=bf(ffn_w_gate), ffn_w_up=bf(ffn_w_up),
             ffn_w_down=bf(ffn_w_down), ple_norm=ple_norm, ple_gate=bf(ple_gate), ple_proj=bf(ple_proj),
             a_w_in=bf(a_w_in), a_w_out=bf(a_w_out), c_w_in=bf(c_w_in), c_w_out=bf(c_w_out),
             a_w_kv_t=bf(a_w_in[0, :, RWKV_PROJ + MOBA_WIDTH:]).T,
             ret_gn_w=ret_gn_w, final_norm=final_norm, rwkv=rwkv)

    yp, kp, vp, rsp, shp, rtp = _run_trunk(
        x_prompt.reshape(B * T, D_MODEL), p_prompt.reshape(DEPTH, B * T, PLE_DIM), B, T, 0,
        lambda z, zk, zv: moba_prompt(z, B, T),
        jnp.zeros((B, RWKV_HEADS, HEAD_DIM, HEAD_DIM), F32), jnp.zeros((B, RWKV_PROJ), F32),
        jnp.zeros((B, RET_HEADS, RET_DK, RET_DV), F32), RWKV_GROUP, min(RWKV_CHUNK, T), W, True)

    past_len = page_table.shape[1] * cache_moba_k.shape[2]
    q_lo = RWKV_PROJ
    ys, ks, vs, rss, shs, rts = _run_trunk(
        x_sample.reshape(Bd * Td, D_MODEL), p_sample.reshape(DEPTH, Bd * Td, PLE_DIM), Bd, Td, past_len,
        lambda z, zk, zv: moba_sample(z[:, q_lo:q_lo + MOBA_WIDTH], zk, zv, cache_moba_k[0],
                                      cache_moba_v[0], page_table, Bd, Td),
        state_rwkv[0], state_rwkv_shift[0], state_ret[0], RWKV_GROUP, min(RWKV_CHUNK, Td), W, False)

    return (yp, ys, kp, vp, ks, vs, rsp, rss, shp, shs, rtp, rts)
```

```python
import functools
import math

import numpy as np
import jax
import jax.numpy as jnp
from jax import lax
from jax.experimental import pallas as pl
from jax.experimental.pallas import tpu as pltpu

F32 = jnp.float32
BF16 = jnp.bfloat16
HI = lax.Precision.HIGHEST

D_MODEL = 1024
DEPTH = 2
PLE_DIM = 256
HEAD_DIM = 64
RWKV_HEADS = 8
RWKV_WIDTH = RWKV_HEADS * HEAD_DIM
W_LORA = 64
A_LORA = 64
G_LORA = 128
RWKV_PROJ = 3 * RWKV_WIDTH + W_LORA + A_LORA + G_LORA
RWKV_LN_EPS = 64e-5
RWKV_CHUNK = 64
RWKV_GROUP = 2
RWKV_SOLVE_BLOCK = 8
MOBA_HEADS = 8
MOBA_WIDTH = MOBA_HEADS * HEAD_DIM
MOBA_BLOCK = 256
MOBA_TOPK = 3
MOBA_QCHUNK = 128
MOBA_KEY_GROUP = 4 * MOBA_BLOCK
A_IN = RWKV_PROJ + 3 * MOBA_WIDTH
RET_HEADS = 4
RET_DK = 256
RET_DV = 512
RET_QK = RET_HEADS * RET_DK
RET_V = RET_HEADS * RET_DV
C_IN = 2 * RET_QK + 2 * RET_V
RET_CHUNK = 128
ROPE_BASE = 10000.0
NORM_EPS = 1e-6
GN_EPS = 1e-5
LOG2E = math.log2(math.e)

LANE = 128
VMEM_LIMIT = 56 * 1024 * 1024

NN = (((1,), (0,)), ((), ()))
NT = (((1,), (1,)), ((), ()))
TN = (((0,), (0,)), ((), ()))


def _params(*sem):
    return pltpu.CompilerParams(dimension_semantics=sem, vmem_limit_bytes=VMEM_LIMIT)


def _rms(x, g):
    return x * lax.rsqrt(jnp.mean(x * x, axis=-1, keepdims=True) + NORM_EPS) * g


ROW_TILE = 1024
ROW_TILE_RESIDENT = 1024
ROW_TILE_FFN = 2048
COL_TILE_MAX = 2048


def _row_tile(n, cap=ROW_TILE):
    t = cap
    while t >= 8:
        if n % t == 0:
            return t
        t //= 2
    raise ValueError(n)


def _col_tile(f):
    for k in range(1, f // LANE + 1):
        if f % k == 0 and (f // k) % LANE == 0 and f // k <= COL_TILE_MAX:
            return f // k
    raise ValueError(f)


def _norm_matmul_kernel(x_ref, g_ref, w_ref, *rest, n_t):
    if n_t:
        wt_ref, o_ref = rest[:2]
        t_refs = rest[2:2 + n_t]
    else:
        o_ref = rest[0]
    xn_ref = rest[-1]

    @pl.when(pl.program_id(1) == 0)
    def _():
        xn_ref[...] = _rms(x_ref[...], g_ref[...]).astype(BF16)

    o_ref[...] = jnp.dot(xn_ref[...], w_ref[...], preferred_element_type=F32)

    if n_t:
        @pl.when(pl.program_id(1) == pl.num_programs(1) - 1)
        def _():
            yt = lax.dot_general(wt_ref[...], xn_ref[...], NT, preferred_element_type=F32)
            cw = yt.shape[0] // n_t
            for k, t_ref in enumerate(t_refs):
                t_ref[0] = yt[k * cw:(k + 1) * cw]


def norm_matmul(x, g, w, transposed=None):
    n, d = x.shape
    f = w.shape[1]
    tm = _row_tile(n)
    tn = _col_tile(f)
    in_specs = [pl.BlockSpec((tm, d), lambda i, j: (i, 0)),
                pl.BlockSpec((1, d), lambda i, j: (0, 0)),
                pl.BlockSpec((d, tn), lambda i, j: (0, j))]
    out_shape = [jax.ShapeDtypeStruct((n, f), F32)]
    out_specs = [pl.BlockSpec((tm, tn), lambda i, j: (i, j))]
    args = [x, g.reshape(1, d), w]
    n_t = 0
    if transposed is not None:
        wt, n_t, b, t = transposed
        cw = wt.shape[0] // n_t
        nt = t // tm
        in_specs.append(pl.BlockSpec(wt.shape, lambda i, j: (0, 0)))
        args.append(wt)
        out_shape += [jax.ShapeDtypeStruct((b, cw, t), F32)] * n_t
        out_specs += [pl.BlockSpec((1, cw, tm), lambda i, j: (i // nt, 0, i % nt))] * n_t
    res = pl.pallas_call(
        functools.partial(_norm_matmul_kernel, n_t=n_t),
        out_shape=out_shape,
        grid=(n // tm, f // tn),
        in_specs=in_specs,
        out_specs=out_specs,
        scratch_shapes=[pltpu.VMEM((tm, d), BF16)],
        compiler_params=_params("parallel", "arbitrary"),
        name="norm_matmul",
    )(*args)
    return res if n_t else res[0]


def _proj_res_kernel(*refs, n_x):
    h_ref = refs[0]
    xs = refs[1:1 + n_x]
    ws = refs[1 + n_x:1 + 2 * n_x]
    o_ref = refs[1 + 2 * n_x]
    acc = h_ref[...]
    for x_ref, w_ref in zip(xs, ws):
        acc = acc + jnp.dot(x_ref[...].astype(BF16), w_ref[...], preferred_element_type=F32)
    o_ref[...] = acc


def proj_residual(h, xs, ws):
    n, d = h.shape
    tm = _row_tile(n, ROW_TILE_RESIDENT)
    in_specs = [pl.BlockSpec((tm, d), lambda i: (i, 0))]
    in_specs += [pl.BlockSpec((tm, x.shape[1]), lambda i: (i, 0)) for x in xs]
    in_specs += [pl.BlockSpec(w.shape, lambda i: (0, 0)) for w in ws]
    return pl.pallas_call(
        functools.partial(_proj_res_kernel, n_x=len(xs)),
        out_shape=jax.ShapeDtypeStruct((n, d), F32),
        grid=(n // tm,),
        in_specs=in_specs,
        out_specs=pl.BlockSpec((tm, d), lambda i: (i, 0)),
        compiler_params=_params("parallel"),
        name="proj_residual",
    )(h, *xs, *ws)


FFN_TILE = 256


def _ffn_kernel(h_ref, g_ref, wg_ref, wu_ref, wd_ref, o_ref, xn_ref):
    @pl.when(pl.program_id(1) == 0)
    def _():
        x = h_ref[...]
        xn_ref[...] = _rms(x, g_ref[...]).astype(BF16)
        o_ref[...] = x

    xn = xn_ref[...]
    a = jnp.dot(xn, wg_ref[...], preferred_element_type=F32)
    b = jnp.dot(xn, wu_ref[...], preferred_element_type=F32)
    t = (a * jax.nn.sigmoid(a)) * b
    o_ref[...] += jnp.dot(t.astype(BF16), wd_ref[...], preferred_element_type=F32)


def ffn_residual(h, g, wg, wu, wd):
    n, d = h.shape
    ff = wg.shape[1]
    tf = FFN_TILE
    tm = _row_tile(n, ROW_TILE_FFN)
    return pl.pallas_call(
        _ffn_kernel,
        out_shape=jax.ShapeDtypeStruct((n, d), F32),
        grid=(n // tm, ff // tf),
        in_specs=[pl.BlockSpec((tm, d), lambda i, j: (i, 0)),
                  pl.BlockSpec((1, d), lambda i, j: (0, 0)),
                  pl.BlockSpec((d, tf), lambda i, j: (0, j)),
                  pl.BlockSpec((d, tf), lambda i, j: (0, j)),
                  pl.BlockSpec((tf, d), lambda i, j: (j, 0))],
        out_specs=pl.BlockSpec((tm, d), lambda i, j: (i, 0)),
        scratch_shapes=[pltpu.VMEM((tm, d), BF16)],
        compiler_params=_params("parallel", "arbitrary"),
        name="ffn_residual",
    )(h, g.reshape(1, d), wg, wu, wd)


def _ple_kernel(h_ref, g_ref, p_ref, wp_ref, wgate_ref, fn_ref, o_ref, *, final):
    x = h_ref[...]
    xn = _rms(x, g_ref[...]).astype(BF16)
    gate = jax.nn.sigmoid(jnp.dot(xn, wgate_ref[...], preferred_element_type=F32))
    pp = jnp.dot(p_ref[...].astype(BF16), wp_ref[...], preferred_element_type=F32)
    hn = x + pp * gate
    if final:
        hn = _rms(hn, fn_ref[...])
    o_ref[...] = hn


def ple_residual(h, g, p, wp, wgate, final_g, final):
    n, d = h.shape
    pd = p.shape[1]
    tm = _row_tile(n, ROW_TILE_RESIDENT)
    return pl.pallas_call(
        functools.partial(_ple_kernel, final=final),
        out_shape=jax.ShapeDtypeStruct((n, d), F32),
        grid=(n // tm,),
        in_specs=[pl.BlockSpec((tm, d), lambda i: (i, 0)),
                  pl.BlockSpec((1, d), lambda i: (0, 0)),
                  pl.BlockSpec((tm, pd), lambda i: (i, 0)),
                  pl.BlockSpec((pd, d), lambda i: (0, 0)),
                  pl.BlockSpec((d, d), lambda i: (0, 0)),
                  pl.BlockSpec((1, d), lambda i: (0, 0))],
        out_specs=pl.BlockSpec((tm, d), lambda i: (i, 0)),
        compiler_params=_params("parallel"),
        name="ple_residual",
    )(h, g.reshape(1, d), p, wp, wgate, final_g.reshape(1, d))


def _softplus(x):
    return jnp.maximum(x, 0.0) + jnp.log1p(jnp.exp(-jnp.abs(x)))


def _split2(x):
    hi = x.astype(BF16)
    return hi, (x - hi.astype(F32)).astype(BF16)


def _split3(x):
    hi = x.astype(BF16)
    r1 = x - hi.astype(F32)
    mid = r1.astype(BF16)
    return hi, mid, (r1 - mid.astype(F32)).astype(BF16)


def _dot3(a, b, dims=NN):
    ah, al = _split2(a)
    bh, bl = _split2(b)
    d = lambda x, y: lax.dot_general(x, y, dims, preferred_element_type=F32)
    return d(ah, bh) + (d(ah, bl) + d(al, bh))


def _dot3_groups(xs, ys, dims=NN):
    return [_dot3(x, y, dims) for x, y in zip(xs, ys)]


def _dot_mask(a, m, mask_left=False):
    d = lambda x: lax.dot_general(m, x, NN, preferred_element_type=F32) if mask_left else \
        lax.dot_general(x, m, NN, preferred_element_type=F32)
    hi, mid, lo = _split3(a)
    return d(hi) + (d(mid) + d(lo))


def _rwkv_kernel(zr_ref, zk_ref, zv_ref, zl_ref, sr_ref, sk_ref, sv_ref, sl_ref,
                 mur_ref, muk_ref, muv_ref, mul_ref, w0_ref, wwa_ref, a0_ref, g2_ref,
                 kkw_ref, kaw_ref, rkw_ref, lnw_ref, lnb_ref, s0_ref,
                 hm_ref, mst_ref, min_ref, tri_ref, bd_ref, lvl_ref, eye_ref,
                 out_ref, s_out_ref, lr_ref, lk_ref, lv_ref, ll_ref, sbd_ref, *, G, C):
    L = G * HEAD_DIM
    R = G * C
    NG = RWKV_HEADS // G
    c = pl.program_id(1)
    wide = R % LANE == 0

    @pl.when(c == 0)
    def _():
        lr_ref[...] = sr_ref[0]
        lk_ref[...] = sk_ref[0]
        lv_ref[...] = sv_ref[0]
        ll_ref[...] = sl_ref[0]
        bdm = bd_ref[:L, :L].astype(F32)
        for gi in range(NG):
            sc = s0_ref[0, :, gi * L:(gi + 1) * L]
            sbd_ref[gi] = jnp.concatenate([sc] * G, axis=0) * bdm

    def shift_mix(x_ref, last_ref, mu_ref):
        x = x_ref[...]
        row = lax.broadcasted_iota(jnp.int32, x.shape, 0)
        prev = jnp.where(row == 0, last_ref[...], pltpu.roll(x, 1, axis=0))
        last_ref[...] = x[C - 1:C, :]
        return x + mu_ref[...] * (prev - x)

    r = shift_mix(zr_ref, lr_ref, mur_ref)
    k = shift_mix(zk_ref, lk_ref, muk_ref)
    v = shift_mix(zv_ref, lv_ref, muv_ref)
    lo = shift_mix(zl_ref, ll_ref, mul_ref)

    wa_in = lo[:, :W_LORA + A_LORA]
    lane = lax.broadcasted_iota(jnp.int32, wa_in.shape, 1)
    wa_in = jnp.where(lane < W_LORA, jnp.tanh(wa_in), wa_in)
    wa = jnp.dot(wa_in, wwa_ref[...], preferred_element_type=F32)
    logw = -_softplus(-(w0_ref[...] + wa[:, :RWKV_WIDTH])) - 0.5
    ld = -jnp.exp(logw)
    a = jax.nn.sigmoid(a0_ref[...] + wa[:, RWKV_WIDTH:])
    g = jnp.dot(jax.nn.sigmoid(lo[:, W_LORA + A_LORA:]), g2_ref[...], preferred_element_type=F32)

    kkv = k * kkw_ref[...]
    k2 = k * (1.0 + (a - 1.0) * kaw_ref[...])
    sums = _dot_mask(jnp.concatenate([kkv * kkv, r * k2 * rkw_ref[...]], axis=0), bd_ref[...])
    kkn = kkv / jnp.maximum(jnp.sqrt(sums[:C]), 1e-12)
    bonus = sums[C:] * v
    an = -kkn
    bn = kkn * a

    cs = _dot_mask(ld, tri_ref[...], mask_left=True)
    e_neg = jnp.exp(-cs)
    e_pos = jnp.exp(cs)
    at = an * jnp.exp(cs - ld)
    rt = r * e_pos
    bt = bn * e_neg
    kt = k2 * e_neg
    pc = e_pos[C - 1:C, :]

    hm = hm_ref[...]
    strict = mst_ref[...] > 0.5
    incl = min_ref[...] > 0.5
    groups = range(NG)
    sls = [slice(gi * L, (gi + 1) * L) for gi in groups]
    stack = lambda x, sl: jnp.concatenate([x[:, sl]] * G, axis=0) * hm
    a_s = [stack(at, sl) for sl in sls]
    r_s = [stack(rt, sl) for sl in sls]
    b_s = [stack(bt, sl) for sl in sls]
    k_s = [stack(kt, sl) for sl in sls]
    v_s = [stack(v, sl) for sl in sls]
    bk = [jnp.concatenate([b_s[i], k_s[i]], axis=0) for i in groups]
    lhs = [jnp.concatenate([a_s[i], r_s[i]], axis=0) for i in groups]
    gdot = _dot3_groups
    if wide:
        m = gdot(lhs, bk, NT)
        m_b = [x[:, :R] for x in m]
        m_k = [x[:, R:] for x in m]
    else:
        m_b, m_k = gdot(lhs, b_s, NT), gdot(lhs, k_s, NT)
    ma_b = [x[:R] for x in m_b]
    a_ak = [jnp.where(strict, x[:R], 0.0) for x in m_k]
    a_rb = [jnp.where(incl, x[R:], 0.0) for x in m_b]
    a_rk = [jnp.where(incl, x[R:], 0.0) for x in m_k]

    sb = [sbd_ref[i] for i in groups]
    w0s = gdot(lhs, sb, NT)
    w0r = [x[R:] for x in w0s]
    akv = gdot(a_ak, v_s)
    rhs = [w0s[i][:R] + akv[i] for i in groups]

    band = lambda i, lv: jnp.where(lvl_ref[lv] > 0.5, ma_b[i], 0.0)
    d = [band(i, 0) for i in groups]
    x = [eye_ref[...] + d[i] for i in groups]
    p2 = gdot(d, d)
    if wide:
        t = [_dot3(p2[i], jnp.concatenate([x[i], p2[i]], axis=1)) for i in groups]
        x = [x[i] + t[i][:, :R] for i in groups]
        p4 = [t[i][:, R:] for i in groups]
    else:
        x, p4 = [x[i] + _dot3(p2[i], x[i]) for i in groups], [_dot3(p2[i], p2[i]) for i in groups]
    p4x = gdot(p4, x)
    tinv = [x[i] + p4x[i] for i in groups]
    n_lv = lvl_ref.shape[0]
    for lv in range(1, n_lv - 1):
        nt = gdot([band(i, lv) for i in groups], tinv)
        tnt = gdot(tinv, nt)
        tinv = [tinv[i] + tnt[i] for i in groups]
    if n_lv == 1:
        u = gdot(tinv, rhs)
    else:
        if wide:
            vb = [_dot3(tinv[i], jnp.concatenate([rhs[i], band(i, n_lv - 1)], axis=1)) for i in groups]
            v0 = [y[:, :L] for y in vb]
            bl = [y[:, L:] for y in vb]
        else:
            v0 = [_dot3(tinv[i], rhs[i]) for i in groups]
            bl = [_dot3(tinv[i], band(i, n_lv - 1)) for i in groups]
        blv = gdot(bl, v0)
        u = [v0[i] + blv[i] for i in groups]
    if wide:
        ruv = gdot([jnp.concatenate([a_rb[i], a_rk[i]], axis=1) for i in groups],
                   [jnp.concatenate([u[i], v_s[i]], axis=0) for i in groups])
        y = [w0r[i] + ruv[i] for i in groups]
    else:
        ru, rv = gdot(a_rb, u), gdot(a_rk, v_s)
        y = [w0r[i] + ru[i] + rv[i] for i in groups]

    ys = []
    for i in groups:
        mu_y = jnp.sum(y[i], axis=-1, keepdims=True) * (1.0 / HEAD_DIM)
        yc = (y[i] - mu_y) * hm
        yn = yc * lax.rsqrt(jnp.sum(yc * yc, axis=-1, keepdims=True) * (1.0 / HEAD_DIM) + RWKV_LN_EPS)
        y_cl = yn[:C]
        for h in range(1, G):
            y_cl = y_cl + yn[h * C:(h + 1) * C]
        ys.append(y_cl)

    uv = [jnp.concatenate([u[i], v_s[i]], axis=0) for i in groups]
    s_new = [(sb[i] + _dot3(uv[i], bk[i], TN)) * pc[:, sls[i]] for i in groups]
    for i in groups:
        sbd_ref[i] = s_new[i]
        s_cat = s_new[i][:HEAD_DIM]
        for h in range(1, G):
            s_cat = s_cat + s_new[i][h * HEAD_DIM:(h + 1) * HEAD_DIM]
        s_out_ref[0, :, sls[i]] = s_cat
    y_all = ys[0] if NG == 1 else jnp.concatenate(ys, axis=1)
    out_ref[...] = (y_all * lnw_ref[...] + lnb_ref[...] + bonus) * g


def _rwkv_tables(G, C):
    L, R = G * HEAD_DIM, G * C
    i = np.arange(R)[:, None]
    j = np.arange(R)[None, :]
    l = np.arange(RWKV_WIDTH)
    same = (i // C) == (j // C)
    hm = ((np.arange(R)[:, None] // C) == (np.arange(L)[None, :] // HEAD_DIM))
    tri = np.arange(C)[:, None] >= np.arange(C)[None, :]
    bd = (l[:, None] // HEAD_DIM) == (l[None, :] // HEAD_DIM)
    f = lambda m, dt=np.float32: jnp.asarray(m.astype(np.float32), dtype=dt)
    assert C % RWKV_SOLVE_BLOCK == 0 and (C // RWKV_SOLVE_BLOCK) & (C // RWKV_SOLVE_BLOCK - 1) == 0
    levels, b = [], RWKV_SOLVE_BLOCK
    prev = np.zeros((R, R), bool)
    while b <= C:
        cur = ((i // b) == (j // b)) & (i > j)
        levels.append(cur & ~prev)
        prev, b = cur, 2 * b
    return [f(hm), f(same & (i > j)), f(same & (i >= j)), f(tri, BF16), f(bd, BF16),
            f(np.stack(levels)), f(np.eye(R, dtype=bool))]


def rwkv_mix(z, shift0, s0_cat, prm, B, T, G, C):
    L = G * HEAD_DIM
    NG = RWKV_HEADS // G
    NC = T // C
    W = RWKV_WIDTH
    lw = W_LORA + A_LORA + G_LORA
    lb = (3 * W) // lw
    row = lambda b, c: b * NC + c
    zspec = lambda off: pl.BlockSpec((C, W), lambda b, c: (row(b, c), off))
    sspec = lambda off: pl.BlockSpec((1, 1, W), lambda b, c: (b, 0, off))
    vspec = lambda off: pl.BlockSpec((1, W), lambda b, c: (0, off))
    const = lambda a: pl.BlockSpec(a.shape, lambda b, c: (0,) * a.ndim)
    tables = _rwkv_tables(G, C)
    mu = prm["mu"].reshape(1, RWKV_PROJ)
    sh = shift0.reshape(B, 1, RWKV_PROJ)
    in_specs = [zspec(0), zspec(1), zspec(2),
                pl.BlockSpec((C, lw), lambda b, c: (row(b, c), lb)),
                sspec(0), sspec(1), sspec(2),
                pl.BlockSpec((1, 1, lw), lambda b, c: (b, 0, lb)),
                vspec(0), vspec(1), vspec(2),
                pl.BlockSpec((1, lw), lambda b, c: (0, lb)),
                vspec(0),
                const(prm["wwa"]),
                vspec(0),
                const(prm["g2"]),
                vspec(0), vspec(0), vspec(0), vspec(0), vspec(0),
                pl.BlockSpec((1, HEAD_DIM, W), lambda b, c: (b, 0, 0))]
    in_specs += [const(t) for t in tables]
    out, s_t = pl.pallas_call(
        functools.partial(_rwkv_kernel, G=G, C=C),
        out_shape=(jax.ShapeDtypeStruct((B * T, W), F32),
                   jax.ShapeDtypeStruct((B, HEAD_DIM, W), F32)),
        grid=(B, NC),
        in_specs=in_specs,
        out_specs=(pl.BlockSpec((C, W), lambda b, c: (row(b, c), 0)),
                   pl.BlockSpec((1, HEAD_DIM, W), lambda b, c: (b, 0, 0))),
        scratch_shapes=[pltpu.VMEM((1, W), F32), pltpu.VMEM((1, W), F32),
                        pltpu.VMEM((1, W), F32), pltpu.VMEM((1, lw), F32),
                        pltpu.VMEM((NG, L, L), F32)],
        compiler_params=_params("parallel", "arbitrary"),
        name="rwkv_mix",
    )(z, z, z, z, sh, sh, sh, sh, mu, mu, mu, mu,
      prm["w0"], prm["wwa"], prm["a0"], prm["g2"], prm["k_k"], prm["k_a"], prm["r_k"],
      prm["ln_w"], prm["ln_b"], s0_cat, *tables)
    return out, s_t


def _pack_wwa(w2, a2):
    top = jnp.concatenate([w2, jnp.zeros_like(w2)], axis=1)
    bot = jnp.concatenate([jnp.zeros_like(a2), a2], axis=1)
    return jnp.concatenate([top, bot], axis=0)


def _top_blocks(gate, n_iota, n_valid, n_blocks, axis=-1):
    neg = -jnp.inf
    valid = n_iota < n_valid
    g = jnp.where(valid, gate, neg)
    sel = jnp.zeros(gate.shape, F32)
    for _ in range(MOBA_TOPK):
        m = jnp.max(g, axis=axis, keepdims=True)
        idx = jnp.min(jnp.where(g == m, n_iota, n_blocks), axis=axis, keepdims=True)
        pick = n_iota == idx
        sel = jnp.where(jnp.logical_and(pick, valid), 1.0, sel)
        g = jnp.where(pick, neg, g)
    return sel


def _top_blocks_by_rank(gate):
    n_iota = lax.broadcasted_iota(jnp.int32, gate.shape, 1)
    rank = jnp.zeros(gate.shape, F32)
    for n in range(gate.shape[1]):
        c = gate[:, n:n + 1]
        beats = jnp.logical_or(c > gate, jnp.logical_and(c == gate, n < n_iota))
        rank = rank + jnp.where(beats, 1.0, 0.0)
    return jnp.where(rank < MOBA_TOPK, 1.0, 0.0)


def _moba_prompt_kernel(q_ref, k_ref, v_ref, o_ref, kbf_ref, vt_ref, kmean_ref, *, T):
    NB = T // MOBA_BLOCK
    QB = MOBA_BLOCK
    KG = MOBA_KEY_GROUP
    BPG = KG // MOBA_BLOCK
    NH = LANE // HEAD_DIM
    blk = pl.program_id(2)
    scale = HEAD_DIM ** -0.5 * LOG2E

    @pl.when(blk == 0)
    def _():
        kf = k_ref[...]
        kbf_ref[...] = kf.astype(BF16)
        kmean_ref[...] = jnp.sum(kf.reshape(NB, MOBA_BLOCK, LANE), axis=1) * (1.0 / MOBA_BLOCK)
        for g in range(T // KG):
            vt_ref[g] = v_ref[g * KG:(g + 1) * KG, :].T.astype(BF16)

    q = q_ref[...]
    lane = lax.broadcasted_iota(jnp.int32, (QB, LANE), 1)
    n_iota = lax.broadcasted_iota(jnp.int32, (NB, QB), 0)
    kmean = kmean_ref[...]
    qbs, sels = [], []
    for hh in range(NH):
        qm = jnp.where((lane // HEAD_DIM) == hh, q, 0.0)
        gate = lax.dot_general(kmean, qm, NT, precision=HI, preferred_element_type=F32)
        qbs.append((qm * scale).astype(BF16))
        sels.append(_top_blocks(gate, n_iota, blk, NB, axis=0))

    kidx = lax.broadcasted_iota(jnp.int32, (MOBA_BLOCK, QB), 0)
    qidx = lax.broadcasted_iota(jnp.int32, (MOBA_BLOCK, QB), 1)
    n_groups = (blk * QB) // KG + 1

    def step(gi, carry, own_group):
        st = pl.multiple_of(gi * KG, KG)
        kg = kbf_ref[pl.ds(st, KG), :]
        vtg = vt_ref[gi]
        heads = range(NH)
        s = [lax.dot_general(kg, qbs[hh], NT, preferred_element_type=F32) for hh in heads]

        def mask(hh):
            parts = []
            for j in range(BPG):
                n = gi * BPG + j
                ok = jnp.max(jnp.where(n_iota == n, sels[hh], 0.0), axis=0, keepdims=True) > 0.5
                if own_group:
                    ok = jnp.logical_or(ok, jnp.logical_and(n == blk, kidx <= qidx))
                parts.append(jnp.where(ok, s[hh][j * MOBA_BLOCK:(j + 1) * MOBA_BLOCK], -jnp.inf))
            return jnp.concatenate(parts, axis=0)

        s = [mask(hh) for hh in heads]
        m_new = [jnp.maximum(carry[hh][0], jnp.max(s[hh], axis=0, keepdims=True)) for hh in heads]
        alpha = [jnp.exp2(carry[hh][0] - m_new[hh]) for hh in heads]
        p = [jnp.exp2(s[hh] - m_new[hh]) for hh in heads]
        l = [alpha[hh] * carry[hh][1] + jnp.sum(p[hh], axis=0, keepdims=True) for hh in heads]
        pv = [jnp.dot(vtg, p[hh].astype(BF16), preferred_element_type=F32) for hh in heads]
        return tuple((m_new[hh], l[hh], alpha[hh] * carry[hh][2] + pv[hh]) for hh in heads)

    init = tuple((jnp.full((1, QB), -jnp.inf, F32), jnp.zeros((1, QB), F32), jnp.zeros((LANE, QB), F32))
                 for _ in range(NH))
    first = step(n_groups - 1, init, True)
    res = lax.fori_loop(1, n_groups, lambda i, carry: step(n_groups - 1 - i, carry, False), first)
    rowi = lax.broadcasted_iota(jnp.int32, (LANE, QB), 0)
    out_t = jnp.zeros((LANE, QB), F32)
    for hh in range(NH):
        _, l, acc = res[hh]
        out_t = jnp.where((rowi // HEAD_DIM) == hh, acc / l, out_t)
    o_ref[...] = out_t.T


def moba_prompt(z, B, T):
    assert T % MOBA_KEY_GROUP == 0 and MOBA_BLOCK % MOBA_QCHUNK == 0
    NB = T // MOBA_BLOCK
    NKG = T // MOBA_KEY_GROUP
    P = MOBA_WIDTH // LANE
    q0 = RWKV_PROJ // LANE
    k0 = (RWKV_PROJ + MOBA_WIDTH) // LANE
    v0 = (RWKV_PROJ + 2 * MOBA_WIDTH) // LANE
    return pl.pallas_call(
        functools.partial(_moba_prompt_kernel, T=T),
        out_shape=jax.ShapeDtypeStruct((B * T, MOBA_WIDTH), F32),
        grid=(B, P, NB),
        in_specs=[pl.BlockSpec((MOBA_BLOCK, LANE), lambda b, p, c: (b * NB + c, q0 + p)),
                  pl.BlockSpec((T, LANE), lambda b, p, c: (b, k0 + p)),
                  pl.BlockSpec((T, LANE), lambda b, p, c: (b, v0 + p))],
        out_specs=pl.BlockSpec((MOBA_BLOCK, LANE), lambda b, p, c: (b * NB + c, p)),
        scratch_shapes=[pltpu.VMEM((T, LANE), BF16), pltpu.VMEM((NKG, LANE, MOBA_KEY_GROUP), BF16),
                        pltpu.VMEM((NB, LANE), F32)],
        compiler_params=_params("parallel", "parallel", "arbitrary"),
        name="moba_prompt",
    )(z, z, z)


def _moba_sample_kernel(pt_ref, q_ref, kn_ref, vn_ref, *refs, n_pages, tq):
    kp = refs[:n_pages]
    vp = refs[n_pages:2 * n_pages]
    hm_ref = refs[2 * n_pages]
    o_ref = refs[2 * n_pages + 1]
    H = MOBA_HEADS
    R = H * tq
    page = kp[0].shape[3]
    ppb = MOBA_BLOCK // page
    n_past = n_pages // ppb
    scale = HEAD_DIM ** -0.5 * LOG2E
    hm = hm_ref[...]
    qs = jnp.concatenate([q_ref[...]] * H, axis=0) * (hm * scale)
    qh, ql = _split2(qs)

    n_iota = lax.broadcasted_iota(jnp.int32, (R, n_past), 1)
    pages = range(n_pages)
    ks = [_split2(kp[j][0].reshape(MOBA_WIDTH, page)) for j in pages]
    s_hi = [jnp.dot(qh, ks[j][0], preferred_element_type=F32) for j in pages]
    s_x1 = [jnp.dot(qh, ks[j][1], preferred_element_type=F32) for j in pages]
    s_x2 = [jnp.dot(ql, ks[j][0], preferred_element_type=F32) for j in pages]
    fine = [s_hi[j] + (s_x1[j] + s_x2[j]) for j in pages]
    gate = jnp.zeros((R, n_past), F32)
    for n in range(n_past):
        blk_el = fine[n * ppb]
        for j in range(1, ppb):
            blk_el = blk_el + fine[n * ppb + j]
        gate = jnp.where(n_iota == n, jnp.sum(blk_el, axis=-1, keepdims=True), gate)
    sel = _top_blocks_by_rank(gate)

    scores = [jnp.where(sel[:, j // ppb:j // ppb + 1] > 0.5, s_hi[j], -jnp.inf) for j in range(n_pages)]
    s_own = lax.dot_general(qh, kn_ref[...].astype(BF16), NT, preferred_element_type=F32)
    qi = lax.broadcasted_iota(jnp.int32, (R, tq), 0) % tq
    ki = lax.broadcasted_iota(jnp.int32, (R, tq), 1)
    s_own = jnp.where(ki <= qi, s_own, -jnp.inf)

    m_el = scores[0]
    for j in range(1, n_pages):
        m_el = jnp.maximum(m_el, scores[j])
    m = jnp.maximum(jnp.max(s_own, axis=-1, keepdims=True), jnp.max(m_el, axis=-1, keepdims=True))
    p_own = jnp.exp2(s_own - m)
    acc = jnp.dot(p_own.astype(BF16), vn_ref[...].astype(BF16), preferred_element_type=F32)
    p = [jnp.exp2(scores[j] - m) for j in pages]
    vt = [vp[j][0].reshape(MOBA_WIDTH, page).astype(BF16) for j in pages]
    pv = [lax.dot_general(p[j].astype(BF16), vt[j], NT, preferred_element_type=F32) for j in pages]
    l_el = p[0]
    for j in range(1, n_pages):
        l_el = l_el + p[j]
        acc = acc + pv[j]
    acc = acc + pv[0]
    l = jnp.sum(p_own, axis=-1, keepdims=True) + jnp.sum(l_el, axis=-1, keepdims=True)
    om = (acc / l) * hm
    out = om[:tq]
    for h in range(1, H):
        out = out + om[h * tq:(h + 1) * tq]
    o_ref[...] = out


def moba_sample(zq, zk, zv, pool_k, pool_v, page_table, B, T):
    n_pages = page_table.shape[1]
    n_pool, page_size = pool_k.shape[0], pool_k.shape[1]
    assert MOBA_BLOCK % page_size == 0 and (n_pages * page_size) % MOBA_BLOCK == 0
    assert T <= MOBA_BLOCK and T % MOBA_QCHUNK != 0 and n_pages * page_size // MOBA_BLOCK >= MOBA_TOPK
    H = MOBA_HEADS
    R = H * T
    pk = pool_k.transpose(0, 2, 3, 1)
    pv = pool_v.transpose(0, 2, 3, 1)
    hm = jnp.asarray(((np.arange(R)[:, None] // T) == (np.arange(MOBA_WIDTH)[None, :] // HEAD_DIM))
                     .astype(np.float32))
    rows = pl.BlockSpec((T, MOBA_WIDTH), lambda b, pt: (b, 0))
    page = lambda j: pl.BlockSpec((1, H, HEAD_DIM, page_size), lambda b, pt: (pt[b, j], 0, 0, 0))
    return pl.pallas_call(
        functools.partial(_moba_sample_kernel, n_pages=n_pages, tq=T),
        out_shape=jax.ShapeDtypeStruct((B * T, MOBA_WIDTH), F32),
        grid_spec=pltpu.PrefetchScalarGridSpec(
            num_scalar_prefetch=1,
            grid=(B,),
            in_specs=[rows, rows, rows] + [page(j) for j in range(n_pages)] * 2
            + [pl.BlockSpec(hm.shape, lambda b, pt: (0, 0))],
            out_specs=rows),
        compiler_params=_params("parallel"),
        name="moba_sample",
    )(page_table, zq, zk, zv, *([pk] * n_pages), *([pv] * n_pages), hm)


def _retention_kernel(q_ref, k_ref, v_ref, g_ref, cos_ref, sin_ref, lg_ref, gnw_ref, s0_ref,
                      o_ref, s_out_ref, *, C):
    c = pl.program_id(1)

    @pl.when(c == 0)
    def _():
        s_out_ref[...] = s0_ref[...]

    half = RET_DK // 2
    heads = range(RET_HEADS)
    cos = cos_ref[...]
    sin = sin_ref[...]

    def rot(x_ref, h):
        x1 = x_ref[:, h * RET_DK:h * RET_DK + half]
        x2 = x_ref[:, h * RET_DK + half:(h + 1) * RET_DK]
        return jnp.concatenate([x1 * cos - x2 * sin, x1 * sin + x2 * cos], axis=-1)

    ii = lax.broadcasted_iota(jnp.int32, (C, C), 0)
    jj = lax.broadcasted_iota(jnp.int32, (C, C), 1)
    diff = jnp.maximum((ii - jj).astype(F32), 0.0)
    it = lax.broadcasted_iota(jnp.int32, (C, LANE), 0).astype(F32)
    lg = [lg_ref[h] for h in heads]
    dmask = [jnp.where(ii >= jj, jnp.exp(diff * lg[h][:, :C]), 0.0) for h in heads]
    cross = [jnp.exp((it + 1.0) * lg[h])[:, :1] for h in heads]
    kdec = [jnp.exp((C - 1.0 - it) * lg[h])[:, :1] for h in heads]
    sdec = [jnp.exp(C * lg[h])[:, :1] for h in heads]

    qb = [rot(q_ref, h).astype(BF16) for h in heads]
    kr = [rot(k_ref, h) * (RET_DK ** -0.5) for h in heads]
    v = [v_ref[:, h * RET_DV:(h + 1) * RET_DV] for h in heads]
    vb = [x.astype(BF16) for x in v]
    s = [s_out_ref[0, h] for h in heads]
    att = [lax.dot_general(qb[h], kr[h].astype(BF16), NT, preferred_element_type=F32) * dmask[h]
           for h in heads]
    y_in = [jnp.dot(att[h].astype(BF16), vb[h], preferred_element_type=F32) for h in heads]
    y_x = [jnp.dot(qb[h], s[h].astype(BF16), preferred_element_type=F32) * cross[h] for h in heads]
    for h in heads:
        s_out_ref[0, h] = s[h] * sdec[h] + lax.dot_general(kr[h] * kdec[h], v[h], TN,
                                                           preferred_element_type=F32)
    for h in heads:
        y = y_in[h] + y_x[h]
        mu = jnp.mean(y, axis=-1, keepdims=True)
        yc = y - mu
        sl = slice(h * RET_DV, (h + 1) * RET_DV)
        yn = yc * lax.rsqrt(jnp.mean(yc * yc, axis=-1, keepdims=True) + GN_EPS) * gnw_ref[:, sl]
        g = g_ref[:, sl]
        o_ref[:, sl] = (g * jax.nn.sigmoid(g)) * yn


def retention_mix(z, s0, gn_w, B, T, pos0):
    C = RET_CHUNK if T % RET_CHUNK == 0 else T
    NC = T // C
    half = RET_DK // 2
    pos = (pos0 + jnp.arange(T)).astype(F32)
    inv = ROPE_BASE ** (-jnp.arange(half, dtype=F32) / half)
    ang = pos[:, None] * inv[None, :]
    log_g = jnp.log1p(-jnp.exp2(-5.0 - jnp.arange(RET_HEADS, dtype=F32)))
    lg = jnp.broadcast_to(log_g[:, None, None], (RET_HEADS, 1, LANE))
    row = lambda b, c: b * NC + c
    state = pl.BlockSpec((1, RET_HEADS, RET_DK, RET_DV), lambda b, c: (b, 0, 0, 0))
    return pl.pallas_call(
        functools.partial(_retention_kernel, C=C),
        out_shape=(jax.ShapeDtypeStruct((B * T, RET_V), F32),
                   jax.ShapeDtypeStruct((B, RET_HEADS, RET_DK, RET_DV), F32)),
        grid=(B, NC),
        in_specs=[pl.BlockSpec((C, RET_QK), lambda b, c: (row(b, c), 0)),
                  pl.BlockSpec((C, RET_QK), lambda b, c: (row(b, c), 1)),
                  pl.BlockSpec((C, RET_V), lambda b, c: (row(b, c), 2 * RET_QK // RET_V)),
                  pl.BlockSpec((C, RET_V), lambda b, c: (row(b, c), 2 * RET_QK // RET_V + 1)),
                  pl.BlockSpec((C, half), lambda b, c: (c, 0)),
                  pl.BlockSpec((C, half), lambda b, c: (c, 0)),
                  pl.BlockSpec((RET_HEADS, 1, LANE), lambda b, c: (0, 0, 0)),
                  pl.BlockSpec((1, RET_V), lambda b, c: (0, 0)),
                  state],
        out_specs=(pl.BlockSpec((C, RET_V), lambda b, c: (row(b, c), 0)), state),
        compiler_params=_params("parallel", "arbitrary"),
        name="retention_mix",
    )(z, z, z, z, jnp.cos(ang), jnp.sin(ang), lg, gn_w.reshape(1, RET_V), s0)


def _run_trunk(x, p, B, T, pos0, moba_fn, rwkv_s0, rwkv_shift0, ret_s0, G, C, W, time_minor_kv):
    h = x
    if time_minor_kv:
        z, k_t, v_t = norm_matmul(h, W["norm_mix"][0], W["a_w_in"][0],
                                  transposed=(W["a_w_kv_t"], 2, B, T))
        rows_out = lambda a: a.reshape(B, MOBA_HEADS, HEAD_DIM, T).transpose(0, 3, 1, 2)[None]
        k_rows, v_rows = rows_out(k_t), rows_out(v_t)
        zk = zv = None
    else:
        z = norm_matmul(h, W["norm_mix"][0], W["a_w_in"][0])
        zk = z[:, RWKV_PROJ + MOBA_WIDTH:RWKV_PROJ + 2 * MOBA_WIDTH]
        zv = z[:, RWKV_PROJ + 2 * MOBA_WIDTH:]
        k_rows = zk.reshape(1, B, T, MOBA_HEADS, HEAD_DIM)
        v_rows = zv.reshape(1, B, T, MOBA_HEADS, HEAD_DIM)
    s0_cat = rwkv_s0.transpose(0, 2, 1, 3).reshape(B, HEAD_DIM, RWKV_WIDTH)
    rwkv_out, s_cat = rwkv_mix(z, rwkv_shift0, s0_cat, W["rwkv"], B, T, G, C)
    rwkv_state = s_cat.reshape(B, HEAD_DIM, RWKV_HEADS, HEAD_DIM).transpose(0, 2, 1, 3)
    moba_out = moba_fn(z, zk, zv)
    shift_t = z.reshape(B, T, A_IN)[:, T - 1, :RWKV_PROJ]
    wo = W["a_w_out"][0]
    h = proj_residual(h, [rwkv_out, moba_out], [wo[:RWKV_WIDTH], wo[RWKV_WIDTH:]])
    h = ffn_residual(h, W["norm_ffn"][0], W["ffn_w_gate"][0], W["ffn_w_up"][0], W["ffn_w_down"][0])
    h = ple_residual(h, W["ple_norm"][0], p[0], W["ple_proj"][0], W["ple_gate"][0],
                     W["final_norm"], final=False)
    z2 = norm_matmul(h, W["norm_mix"][1], W["c_w_in"][0])
    gated, ret_state = retention_mix(z2, ret_s0, W["ret_gn_w"][0], B, T, pos0)
    h = proj_residual(h, [gated], [W["c_w_out"][0]])
    h = ffn_residual(h, W["norm_ffn"][1], W["ffn_w_gate"][1], W["ffn_w_up"][1], W["ffn_w_down"][1])
    y = ple_residual(h, W["ple_norm"][1], p[1], W["ple_proj"][1], W["ple_gate"][1],
                     W["final_norm"], final=True)
    return (y.reshape(B, T, D_MODEL), k_rows, v_rows, rwkv_state[None],
            shift_t[None], ret_state[None])


def kernel(x_prompt, x_sample, cache_moba_k, cache_moba_v, state_rwkv, state_rwkv_shift, state_ret, page_table, p_prompt, p_sample, norm_mix, norm_ffn, ffn_w_gate, ffn_w_up, ffn_w_down, ple_norm, ple_gate, ple_proj, a_w_in, rwkv_mu, rwkv_w0, rwkv_w2, rwkv_a0, rwkv_a2, rwkv_g2, rwkv_k_k, rwkv_k_a, rwkv_r_k, rwkv_ln_w, rwkv_ln_b, a_w_out, c_w_in, ret_gn_w, c_w_out, final_norm):
    assert norm_mix.shape[0] == DEPTH == 2
    B, T, _ = x_prompt.shape
    Bd, Td, _ = x_sample.shape
    bf = lambda w: w.astype(BF16)
    row = lambda a: a.reshape(1, RWKV_WIDTH)
    rwkv = dict(mu=rwkv_mu[0], w0=row(rwkv_w0[0]), a0=row(rwkv_a0[0]), g2=rwkv_g2[0],
                k_k=row(rwkv_k_k[0]), k_a=row(rwkv_k_a[0]), r_k=row(rwkv_r_k[0]),
                ln_w=row(rwkv_ln_w[0]), ln_b=row(rwkv_ln_b[0]),
                wwa=_pack_wwa(rwkv_w2[0], rwkv_a2[0]))
    W = dict(norm_mix=norm_mix, norm_ffn=norm_ffn, ffn_w_gate=bf(ffn_w_gate), ffn_w_up=bf(ffn_w_up),
             ffn_w_down=bf(ffn_w_down), ple_norm=ple_norm, ple_gate=bf(ple_gate), ple_proj=bf(ple_proj),
             a_w_in=bf(a_w_in), a_w_out=bf(a_w_out), c_w_in=bf(c_w_in), c_w_out=bf(c_w_out),
             a_w_kv_t=bf(a_w_in[0, :, RWKV_PROJ + MOBA_WIDTH:]).T,
             ret_gn_w=ret_gn_w, final_norm=final_norm, rwkv=rwkv)

    yp, kp, vp, rsp, shp, rtp = _run_trunk(
        x_prompt.reshape(B * T, D_MODEL), p_prompt.reshape(DEPTH, B * T, PLE_DIM), B, T, 0,
        lambda z, zk, zv: moba_prompt(z, B, T),
        jnp.zeros((B, RWKV_HEADS, HEAD_DIM, HEAD_DIM), F32), jnp.zeros((B, RWKV_PROJ), F32),
        jnp.zeros((B, RET_HEADS, RET_DK, RET_DV), F32), RWKV_GROUP, min(RWKV_CHUNK, T), W, True)

    past_len = page_table.shape[1] * cache_moba_k.shape[2]
    q_lo = RWKV_PROJ
    ys, ks, vs, rss, shs, rts = _run_trunk(
        x_sample.reshape(Bd * Td, D_MODEL), p_sample.reshape(DEPTH, Bd * Td, PLE_DIM), Bd, Td, past_len,
        lambda z, zk, zv: moba_sample(z[:, q_lo:q_lo + MOBA_WIDTH], zk, zv, cache_moba_k[0],
                                      cache_moba_v[0], page_table, Bd, Td),
        state_rwkv[0], state_rwkv_shift[0], state_ret[0], RWKV_GROUP, min(RWKV_CHUNK, Td), W, False)

    return (yp, ys, kp, vp, ks, vs, rsp, rss, shp, shs, rtp, rts)
```
